```python
import math
import jax, jax.numpy as jnp
from jax import lax
import numpy as np

D_MODEL = 1024
BATCH = 4
SEQ = 8192
DEPTH = 1

N_META = 16
BLOCK = 128
N_PAD = BLOCK - N_META
D_MIX = D_MODEL
DIFF_WIDTH = D_MIX // 2
DIFF_V_DIM = 128
DIFF_QK_DIM = DIFF_V_DIM // 2
DIFF_HEADS = DIFF_WIDTH // DIFF_V_DIM
FOX_WIDTH = D_MIX - DIFF_WIDTH
FOX_HEAD_DIM = 64
FOX_HEADS = FOX_WIDTH // FOX_HEAD_DIM
ROPE_DIMS = DIFF_QK_DIM // 4
ROPE_THETA = 500000.0
D_FF = ((8 * D_MODEL // 3 + 255) // 256) * 256
RMS_EPS = 1e-6
SUBLN_EPS = 1e-5
NEG_INF = -1e30
DIFF_Q_COLS = DIFF_HEADS * 2 * DIFF_QK_DIM
DIFF_K_COLS = DIFF_HEADS * 2 * DIFF_QK_DIM
DIFF_V_COLS = DIFF_HEADS * DIFF_V_DIM
FOX_Q_COLS = FOX_HEADS * FOX_HEAD_DIM
FOX_K_COLS = FOX_HEADS * FOX_HEAD_DIM
FOX_V_COLS = FOX_HEADS * FOX_HEAD_DIM
FOX_F_COLS = FOX_HEADS
IN_COLS = DIFF_Q_COLS + DIFF_K_COLS + DIFF_V_COLS + FOX_Q_COLS + FOX_K_COLS + FOX_V_COLS + FOX_F_COLS
SPLITS = list(np.cumsum([DIFF_Q_COLS, DIFF_K_COLS, DIFF_V_COLS, FOX_Q_COLS, FOX_K_COLS, FOX_V_COLS]))

kernel_name = "hybrid_diffattn_fox_macaron_meta"


def lambda_init_fn(layer_idx):
    return 0.8 - 0.6 * math.exp(-0.3 * layer_idx)


def rmsnorm(x, g, eps=RMS_EPS):
    xf = x.astype(jnp.float32)
    y = xf * lax.rsqrt(jnp.mean(xf * xf, axis=-1, keepdims=True) + eps)
    return (y * g.astype(jnp.float32)).astype(x.dtype)


def swiglu(x, w_gate_up, w_down):
    g, u = jnp.split(x @ w_gate_up, 2, axis=-1)
    return (jax.nn.silu(g) * u) @ w_down


def apply_partial_rope(t, cos, sin):
    tf = t.astype(jnp.float32)
    half = ROPE_DIMS // 2
    x1 = tf[..., :half]
    x2 = tf[..., half:ROPE_DIMS]
    rot = jnp.concatenate([x1 * cos - x2 * sin, x2 * cos + x1 * sin, tf[..., ROPE_DIMS:]], axis=-1)
    return rot.astype(t.dtype)


def hybrid_mixer(hn, w_in, b_forget, lam_q1, lam_k1, lam_q2, lam_k2, subln_g, w_out, lambda_init):
    B, L, _ = hn.shape
    Lp = L + N_PAD
    n_blocks = Lp // BLOCK
    proj = hn @ w_in
    dq, dk, dv, fq, fk, fv, fl = jnp.split(proj, SPLITS, axis=-1)
    pad = lambda t: jnp.pad(t, ((0, 0), (N_PAD, 0), (0, 0)))

    pos = (jnp.arange(Lp, dtype=jnp.int32) - N_PAD).astype(jnp.float32)
    inv_freq = jnp.power(ROPE_THETA, -jnp.arange(0, ROPE_DIMS, 2, dtype=jnp.float32) / ROPE_DIMS)
    ang = pos[:, None] * inv_freq[None, :]
    cos = jnp.cos(ang)[None, :, None, None, :]
    sin = jnp.sin(ang)[None, :, None, None, :]

    dq = apply_partial_rope(pad(dq).reshape(B, Lp, DIFF_HEADS, 2, DIFF_QK_DIM), cos, sin).transpose(0, 2, 3, 1, 4)
    dk = apply_partial_rope(pad(dk).reshape(B, Lp, DIFF_HEADS, 2, DIFF_QK_DIM), cos, sin).transpose(0, 2, 3, 1, 4)
    dv = pad(dv).reshape(B, Lp, DIFF_HEADS, DIFF_V_DIM).transpose(0, 2, 1, 3)
    lam = (jnp.exp(jnp.sum(lam_q1.astype(jnp.float32) * lam_k1.astype(jnp.float32)))
           - jnp.exp(jnp.sum(lam_q2.astype(jnp.float32) * lam_k2.astype(jnp.float32)))
           + lambda_init)

    fq = pad(fq).reshape(B, Lp, FOX_HEADS, FOX_HEAD_DIM).transpose(0, 2, 1, 3)
    fk = pad(fk).reshape(B, Lp, FOX_HEADS, FOX_HEAD_DIM).transpose(0, 2, 1, 3)
    fv = pad(fv).reshape(B, Lp, FOX_HEADS, FOX_HEAD_DIM).transpose(0, 2, 1, 3)
    log_f = jax.nn.log_sigmoid(fl.astype(jnp.float32) + b_forget.astype(jnp.float32))
    log_f = jnp.pad(log_f, ((0, 0), (N_PAD, 0), (0, 0)))
    cum = jnp.cumsum(log_f, axis=1).transpose(0, 2, 1)

    diff_scale = DIFF_QK_DIM ** -0.5
    fox_scale = FOX_HEAD_DIM ** -0.5
    kidx = jnp.arange(Lp, dtype=jnp.int32)

    def block(i):
        start = i * BLOCK
        qidx = start + jnp.arange(BLOCK, dtype=jnp.int32)
        valid = (kidx[None, :] <= qidx[:, None]) & (kidx[None, :] >= N_PAD)
        q_d = lax.dynamic_slice_in_dim(dq, start, BLOCK, axis=3)
        s_d = jnp.einsum('bhcqd,bhckd->bhcqk', q_d, dk, preferred_element_type=jnp.float32) * diff_scale
        p_d = jax.nn.softmax(jnp.where(valid, s_d, NEG_INF), axis=-1)
        a_d = p_d[:, :, 0] - lam * p_d[:, :, 1]
        o_d = jnp.einsum('bhqk,bhkd->bhqd', a_d.astype(dv.dtype), dv)
        o_d = rmsnorm(o_d, subln_g, SUBLN_EPS) * (1.0 - lambda_init)
        q_f = lax.dynamic_slice_in_dim(fq, start, BLOCK, axis=2)
        c_q = lax.dynamic_slice_in_dim(cum, start, BLOCK, axis=2)
        s_f = (jnp.einsum('bhqd,bhkd->bhqk', q_f, fk, preferred_element_type=jnp.float32) * fox_scale
               + c_q[..., :, None] - cum[..., None, :])
        p_f = jax.nn.softmax(jnp.where(valid, s_f, NEG_INF), axis=-1)
        o_f = jnp.einsum('bhqk,bhkd->bhqd', p_f.astype(fv.dtype), fv)
        return jnp.concatenate([
            o_d.transpose(0, 2, 1, 3).reshape(B, BLOCK, DIFF_WIDTH),
            o_f.transpose(0, 2, 1, 3).reshape(B, BLOCK, FOX_WIDTH)], axis=-1)

    outs = lax.map(block, jnp.arange(n_blocks, dtype=jnp.int32))
    o = outs.transpose(1, 0, 2, 3).reshape(B, Lp, D_MIX)[:, N_PAD:]
    return o @ w_out


def setup_inputs(seed: int = 0) -> dict:
    key = jax.random.key(seed)
    ks = jax.random.split(key, 20)
    f32 = jnp.float32
    nrm = lambda k, shape, scale: jax.random.normal(k, shape, f32) * scale
    gain = lambda k, shape: 1.0 + 0.02 * jax.random.normal(k, shape, f32)
    return {
        "x": jax.random.normal(ks[0], (BATCH, SEQ, D_MODEL), f32),
        "meta_tokens": nrm(ks[1], (N_META, D_MODEL), 1.0),
        "ffn1_norm_g": gain(ks[2], (DEPTH, D_MODEL)),
        "ffn1_w_gate_up": nrm(ks[3], (DEPTH, D_MODEL, 2 * D_FF), D_MODEL ** -0.5),
        "ffn1_w_down": nrm(ks[4], (DEPTH, D_FF, D_MODEL), D_FF ** -0.5),
        "mix_norm_g": gain(ks[5], (DEPTH, D_MODEL)),
        "w_in": nrm(ks[6], (DEPTH, D_MODEL, IN_COLS), D_MODEL ** -0.5),
        "b_forget": 1.0 + 0.1 * jax.random.normal(ks[7], (DEPTH, FOX_HEADS), f32),
        "lam_q1": nrm(ks[8], (DEPTH, DIFF_QK_DIM), 0.1),
        "lam_k1": nrm(ks[9], (DEPTH, DIFF_QK_DIM), 0.1),
        "lam_q2": nrm(ks[10], (DEPTH, DIFF_QK_DIM), 0.1),
        "lam_k2": nrm(ks[11], (DEPTH, DIFF_QK_DIM), 0.1),
        "diff_subln_g": gain(ks[12], (DEPTH, DIFF_V_DIM)),
        "w_out": nrm(ks[13], (DEPTH, D_MIX, D_MODEL), D_MIX ** -0.5),
        "ffn2_norm_g": gain(ks[14], (DEPTH, D_MODEL)),
        "ffn2_w_gate_up": nrm(ks[15], (DEPTH, D_MODEL, 2 * D_FF), D_MODEL ** -0.5),
        "ffn2_w_down": nrm(ks[16], (DEPTH, D_FF, D_MODEL), D_FF ** -0.5),
        "final_norm_g": gain(ks[17], (D_MODEL,)),
    }


def reference(x, meta_tokens, ffn1_norm_g, ffn1_w_gate_up, ffn1_w_down, mix_norm_g, w_in, b_forget,
              lam_q1, lam_k1, lam_q2, lam_k2, diff_subln_g, w_out, ffn2_norm_g, ffn2_w_gate_up,
              ffn2_w_down, final_norm_g):
    B = x.shape[0]
    meta = jnp.broadcast_to(meta_tokens.astype(x.dtype)[None], (B, N_META, D_MODEL))
    h = jnp.concatenate([meta, x], axis=1)
    for layer in range(DEPTH):
        h = h + 0.5 * swiglu(rmsnorm(h, ffn1_norm_g[layer]), ffn1_w_gate_up[layer], ffn1_w_down[layer])
        h = h + hybrid_mixer(rmsnorm(h, mix_norm_g[layer]), w_in[layer], b_forget[layer],
                             lam_q1[layer], lam_k1[layer], lam_q2[layer], lam_k2[layer],
                             diff_subln_g[layer], w_out[layer], lambda_init_fn(layer))
        h = h + 0.5 * swiglu(rmsnorm(h, ffn2_norm_g[layer]), ffn2_w_gate_up[layer], ffn2_w_down[layer])
    return rmsnorm(h, final_norm_g)[:, N_META:]
```

```python
import functools
import math

import numpy as np
import jax
import jax.numpy as jnp
from jax import lax
from jax.experimental import pallas as pl
from jax.experimental.pallas import tpu as pltpu

F32 = jnp.float32
BF16 = jnp.bfloat16

N_META = 16
DIFF_HEADS = 4
DIFF_V_DIM = 128
DIFF_QK_DIM = 64
FOX_HEADS = 8
FOX_HEAD_DIM = 64
ROPE_DIMS = 16
ROPE_THETA = 500000.0
RMS_EPS = 1e-6
SUBLN_EPS = 1e-5
NEG_INF = -1e30
QK_SCALE = 0.125

N_QMAPS = 2 * DIFF_HEADS + FOX_HEADS
N_KBLOCKS = DIFF_HEADS + FOX_HEADS
BF16_SUBLANES = 16
VD_ROWS = DIFF_V_DIM + BF16_SUBLANES
VF_ROWS = FOX_HEAD_DIM + BF16_SUBLANES
META_TILE = 128
FFN_CHUNK = 256
ROW_TILE = 512
ATTN_TILE = 512
VMEM_LIMIT = 56 * 1024 * 1024


def _lambda_init(layer_idx):
    return 0.8 - 0.6 * math.exp(-0.3 * layer_idx)


def _rms(x, g, eps):
    ms = jnp.mean(x * x, axis=-1, keepdims=True)
    return x * lax.rsqrt(ms + eps) * g


def _split3(x):
    hi = x.astype(BF16)
    r = x - hi.astype(F32)
    mid = r.astype(BF16)
    lo = (r - mid.astype(F32)).astype(BF16)
    return hi, mid, lo


def _const_spec(shape):
    nd = len(shape)
    return pl.BlockSpec(shape, lambda *_: (0,) * nd, pipeline_mode=pl.Buffered(1))


def _swiglu_half_step(h, g_ref, wgu_ref, wd_ref):
    d_ff = wd_ref.shape[0]
    hn = _rms(h, g_ref[...], RMS_EPS).astype(BF16)
    acc = jnp.zeros(h.shape, F32)
    for c in range(d_ff // FFN_CHUNK):
        lo, hi = c * FFN_CHUNK, (c + 1) * FFN_CHUNK
        gate = jnp.dot(hn, wgu_ref[:, lo:hi], preferred_element_type=F32)
        up = jnp.dot(hn, wgu_ref[:, d_ff + lo:d_ff + hi], preferred_element_type=F32)
        act = (gate / (1.0 + jnp.exp(-gate)) * up).astype(BF16)
        acc = acc + jnp.dot(act, wd_ref[lo:hi, :], preferred_element_type=F32)
    return h + 0.5 * acc


def _ffn1_kernel(h_ref, g_ref, wgu_ref, wd_ref, o_ref):
    o_ref[...] = _swiglu_half_step(h_ref[...], g_ref, wgu_ref, wd_ref)


def _ffn1(h, g, wgu, wd, tm):
    rows, d = h.shape
    return pl.pallas_call(
        _ffn1_kernel,
        grid=(rows // tm,),
        in_specs=[pl.BlockSpec((tm, d), lambda i: (i, 0)),
                  _const_spec(g.shape), _const_spec(wgu.shape), _const_spec(wd.shape)],
        out_specs=pl.BlockSpec((tm, d), lambda i: (i, 0)),
        out_shape=jax.ShapeDtypeStruct((rows, d), F32),
        compiler_params=pltpu.CompilerParams(dimension_semantics=("parallel",),
                                             vmem_limit_bytes=VMEM_LIMIT),
        name="ffn1",
    )(h, g, wgu, wd)


def _tail_kernel(o_ref, h_ref, wout_ref, g_ref, wgu_ref, wd_ref, gf_ref, out_ref):
    h2 = h_ref[...] + jnp.dot(o_ref[...], wout_ref[...], preferred_element_type=F32)
    h3 = _swiglu_half_step(h2, g_ref, wgu_ref, wd_ref)
    out_ref[...] = _rms(h3, gf_ref[...], RMS_EPS)


def _tail(o, h1, wout, g, wgu, wd, gf, tm):
    rows, d = h1.shape
    return pl.pallas_call(
        _tail_kernel,
        grid=(rows // tm,),
        in_specs=[pl.BlockSpec((tm, d), lambda i: (i, 0)),
                  pl.BlockSpec((tm, d), lambda i: (i, 0)),
                  _const_spec(wout.shape), _const_spec(g.shape), _const_spec(wgu.shape),
                  _const_spec(wd.shape), _const_spec(gf.shape)],
        out_specs=pl.BlockSpec((tm, d), lambda i: (i, 0)),
        out_shape=jax.ShapeDtypeStruct((rows, d), F32),
        compiler_params=pltpu.CompilerParams(dimension_semantics=("parallel",),
                                             vmem_limit_bytes=VMEM_LIMIT),
        name="tail",
    )(o, h1, wout, g, wgu, wd, gf)


def _proj_kernel(h_ref, g_ref, wrows_ref, wt_ref, bf_ref, cc_ref, ss_ref, cost_ref, sint_ref,
                 carry0_ref, place_ref,
                 qt_ref, k_ref, vd_ref, vf_ref, carry_out_ref, carry_scr, *, n_valid):
    tm = h_ref.shape[1]

    @pl.when(pl.program_id(1) == 0)
    def _():
        carry_scr[...] = carry0_ref[...]

    hn = _rms(h_ref[0], g_ref[...], RMS_EPS).astype(BF16)
    kr = jnp.dot(hn, wrows_ref[...], preferred_element_type=F32)
    pt = lax.dot_general(wt_ref[...], hn, (((1,), (1,)), ((), ())),
                         preferred_element_type=F32)

    cc = cc_ref[...]
    ss = ss_ref[...]
    n_dk = DIFF_HEADS * 128
    for p in range(DIFF_HEADS):
        kd = kr[:, p * 128:(p + 1) * 128]
        ks = kr[:, n_dk + p * 128:n_dk + (p + 1) * 128]
        k_ref[0, p] = (kd * cc + ks * ss).astype(BF16)

    fk0 = 2 * n_dk
    fl0 = fk0 + FOX_HEADS * 128
    fl = kr[:, fl0:fl0 + 128] + bf_ref[...]
    logf = jnp.minimum(fl, 0.0) - jnp.log(1.0 + jnp.exp(-jnp.abs(fl)))
    lane = lax.broadcasted_iota(jnp.int32, (tm, 128), 1)
    keep = lane < FOX_HEADS
    if n_valid < tm:
        row = lax.broadcasted_iota(jnp.int32, (tm, 128), 0)
        keep = jnp.logical_and(keep, row < n_valid)
    logf = jnp.where(keep, logf, 0.0)
    parts = jnp.concatenate(_split3(logf), axis=1)
    r_i = lax.broadcasted_iota(jnp.int32, (tm, tm), 0)
    c_i = lax.broadcasted_iota(jnp.int32, (tm, tm), 1)
    tri = jnp.where(r_i >= c_i, 1.0, 0.0).astype(BF16)
    cs = jnp.dot(tri, parts, preferred_element_type=F32)
    cum = cs[:, 0:128] + cs[:, 128:256] + cs[:, 256:384] + carry_scr[0:1, :]
    carry_scr[...] = jnp.broadcast_to(cum[tm - 1:tm, :], carry_scr.shape)
    carry_out_ref[0] = carry_scr[...]
    cparts = jnp.concatenate(_split3(cum), axis=1)
    aug = jnp.dot(cparts, place_ref[...], preferred_element_type=F32)
    for hh in range(FOX_HEADS):
        kf = kr[:, fk0 + hh * 128:fk0 + (hh + 1) * 128] + aug[:, hh * 128:(hh + 1) * 128]
        k_ref[0, DIFF_HEADS + hh] = kf.astype(BF16)

    cost = cost_ref[...]
    sint = sint_ref[...]
    half = ROPE_DIMS // 2
    zeros64 = jnp.zeros((64, tm), BF16)
    for h in range(DIFF_HEADS):
        for c in range(2):
            base = h * 128 + c * 64
            x1 = pt[base:base + half]
            x2 = pt[base + half:base + ROPE_DIMS]
            blk = jnp.concatenate([x1 * cost - x2 * sint, x2 * cost + x1 * sint,
                                   pt[base + ROPE_DIMS:base + 64]], axis=0).astype(BF16)
            u = 2 * h + c
            qt_ref[0, u, c * 64:(c + 1) * 64, :] = blk
            qt_ref[0, u, (1 - c) * 64:(2 - c) * 64, :] = zeros64
    fq0 = DIFF_HEADS * 128
    rows64 = lax.broadcasted_iota(jnp.int32, (64, tm), 0)
    neg_rows = jnp.where(rows64 < 3, -1.0, 0.0).astype(BF16)
    for hh in range(FOX_HEADS):
        u = 2 * DIFF_HEADS + hh
        qt_ref[0, u, 0:64, :] = pt[fq0 + hh * 64:fq0 + (hh + 1) * 64].astype(BF16)
        qt_ref[0, u, 64:128, :] = neg_rows

    ones_rows = jnp.ones((BF16_SUBLANES, tm), BF16)
    dv0 = fq0 + FOX_HEADS * 64
    for h in range(DIFF_HEADS):
        vd_ref[0, h, 0:DIFF_V_DIM, :] = pt[dv0 + h * 128:dv0 + (h + 1) * 128].astype(BF16)
        vd_ref[0, h, DIFF_V_DIM:VD_ROWS, :] = ones_rows
    fv0 = dv0 + DIFF_HEADS * 128
    for hh in range(FOX_HEADS):
        vf_ref[0, hh, 0:FOX_HEAD_DIM, :] = pt[fv0 + hh * 64:fv0 + (hh + 1) * 64].astype(BF16)
        vf_ref[0, hh, FOX_HEAD_DIM:VF_ROWS, :] = ones_rows


def _proj(h, g, wrows, wt, bfg, cc, ss, cost, sint, carry0, place, tm, n_valid):
    b, s, d = h.shape
    nt = s // tm
    out_shape = (
        jax.ShapeDtypeStruct((b, N_QMAPS, 128, s), BF16),
        jax.ShapeDtypeStruct((b, N_KBLOCKS, s, 128), BF16),
        jax.ShapeDtypeStruct((b, DIFF_HEADS, VD_ROWS, s), BF16),
        jax.ShapeDtypeStruct((b, FOX_HEADS, VF_ROWS, s), BF16),
        jax.ShapeDtypeStruct((b, 8, 128), F32),
    )
    return pl.pallas_call(
        functools.partial(_proj_kernel, n_valid=n_valid),
        grid=(b, nt),
        in_specs=[pl.BlockSpec((1, tm, d), lambda i, t: (i, t, 0)),
                  _const_spec(g.shape), _const_spec(wrows.shape), _const_spec(wt.shape),
                  _const_spec(bfg.shape),
                  pl.BlockSpec((tm, 128), lambda i, t: (t, 0)),
                  pl.BlockSpec((tm, 128), lambda i, t: (t, 0)),
                  pl.BlockSpec((8, tm), lambda i, t: (0, t)),
                  pl.BlockSpec((8, tm), lambda i, t: (0, t)),
                  _const_spec(carry0.shape), _const_spec(place.shape)],
        out_specs=(pl.BlockSpec((1, N_QMAPS, 128, tm), lambda i, t: (i, 0, 0, t)),
                   pl.BlockSpec((1, N_KBLOCKS, tm, 128), lambda i, t: (i, 0, t, 0)),
                   pl.BlockSpec((1, DIFF_HEADS, VD_ROWS, tm), lambda i, t: (i, 0, 0, t)),
                   pl.BlockSpec((1, FOX_HEADS, VF_ROWS, tm), lambda i, t: (i, 0, 0, t)),
                   pl.BlockSpec((1, 8, 128), lambda i, t: (i, 0, 0))),
        out_shape=out_shape,
        scratch_shapes=[pltpu.VMEM((8, 128), F32)],
        compiler_params=pltpu.CompilerParams(dimension_semantics=("parallel", "arbitrary"),
                                             vmem_limit_bytes=VMEM_LIMIT),
        name="proj",
    )(h, g, wrows, wt, bfg, cc, ss, cost, sint, carry0, place)


def _attn_kernel(qi_ref, kj_ref,
                 qt_ref, k_ref, vd_ref, vf_ref, km_ref, vdm_ref, vfm_ref,
                 lq1_ref, lk1_ref, lq2_ref, lk2_ref, subg_ref,
                 o_ref, m_scr, accd_scr, accf_scr, *, lambda_init):
    s_idx = pl.program_id(1)
    qi = qi_ref[s_idx]
    kj = kj_ref[s_idx]
    tq = qt_ref.shape[3]
    tk = k_ref.shape[2]

    def update(u, acc_scr, a, k_tile, v_t, mask):
        st = jnp.dot(k_tile, qt_ref[0, u], preferred_element_type=F32)
        if mask is not None:
            st = jnp.where(mask, st, NEG_INF)
        m_old = m_scr[u]
        m_new = jnp.maximum(m_old, jnp.max(st, axis=0, keepdims=True))
        alpha = jnp.exp(m_old - m_new)
        p = jnp.exp(st - m_new).astype(BF16)
        pv = jnp.dot(v_t, p, preferred_element_type=F32)
        acc_scr[a] = acc_scr[a] * alpha + pv
        m_scr[u] = m_new

    def sweep(k_blocks, vd_blocks, vf_blocks, mask):
        def diff_body(h, carry):
            for c in range(2):
                update(2 * h + c, accd_scr, 2 * h + c, k_blocks(h), vd_blocks(h), mask)
            return carry

        def fox_body(i, carry):
            for c in range(2):
                hh = 2 * i + c
                update(2 * DIFF_HEADS + hh, accf_scr, hh, k_blocks(DIFF_HEADS + hh),
                       vf_blocks(hh), mask)
            return carry

        lax.fori_loop(0, DIFF_HEADS, diff_body, 0)
        lax.fori_loop(0, FOX_HEADS // 2, fox_body, 0)

    @pl.when(kj == 0)
    def _():
        m_scr[...] = jnp.full(m_scr.shape, NEG_INF, F32)
        accd_scr[...] = jnp.zeros(accd_scr.shape, F32)
        accf_scr[...] = jnp.zeros(accf_scr.shape, F32)
        key = lax.broadcasted_iota(jnp.int32, (META_TILE, tq), 0)
        sweep(lambda p: km_ref[p], lambda h: vdm_ref[h], lambda h: vfm_ref[h], key < N_META)

    @pl.when(kj < qi)
    def _():
        sweep(lambda p: k_ref[0, p], lambda h: vd_ref[0, h], lambda h: vf_ref[0, h], None)

    @pl.when(kj == qi)
    def _():
        key = lax.broadcasted_iota(jnp.int32, (tk, tq), 0)
        qry = lax.broadcasted_iota(jnp.int32, (tk, tq), 1)
        sweep(lambda p: k_ref[0, p], lambda h: vd_ref[0, h], lambda h: vf_ref[0, h], key <= qry)

        lam = (jnp.exp(jnp.sum(lq1_ref[...] * lk1_ref[...], axis=1, keepdims=True))
               - jnp.exp(jnp.sum(lq2_ref[...] * lk2_ref[...], axis=1, keepdims=True))
               + lambda_init)
        for h in range(DIFF_HEADS):
            a1 = accd_scr[2 * h]
            a2 = accd_scr[2 * h + 1]
            o = (a1[0:DIFF_V_DIM] / a1[DIFF_V_DIM:DIFF_V_DIM + 1]
                 - lam * (a2[0:DIFF_V_DIM] / a2[DIFF_V_DIM:DIFF_V_DIM + 1]))
            ms = jnp.mean(o * o, axis=0, keepdims=True)
            o = o * lax.rsqrt(ms + SUBLN_EPS) * subg_ref[...] * (1.0 - lambda_init)
            o_ref[0, :, h * 128:(h + 1) * 128] = o.T.astype(o_ref.dtype)
        for i in range(FOX_HEADS // 2):
            a0 = accf_scr[2 * i]
            a1 = accf_scr[2 * i + 1]
            o = jnp.concatenate(
                [a0[0:FOX_HEAD_DIM] / a0[FOX_HEAD_DIM:FOX_HEAD_DIM + 1],
                 a1[0:FOX_HEAD_DIM] / a1[FOX_HEAD_DIM:FOX_HEAD_DIM + 1]], axis=0)
            c0 = DIFF_HEADS * DIFF_V_DIM + i * 128
            o_ref[0, :, c0:c0 + 128] = o.T.astype(o_ref.dtype)


def _attn(qt, k, vd, vf, km, vdm, vfm, lq1, lk1, lq2, lk2, subg, lambda_init, tile):
    b, _, _, s = qt.shape
    nq = s // tile
    qi_tab = np.concatenate([np.full(i + 1, i, np.int32) for i in range(nq)])
    kj_tab = np.concatenate([np.arange(i + 1, dtype=np.int32) for i in range(nq)])
    n_steps = int(qi_tab.shape[0])
    d_mix = DIFF_HEADS * DIFF_V_DIM + FOX_HEADS * FOX_HEAD_DIM

    def const(shape):
        nd = len(shape)
        return pl.BlockSpec(shape, lambda *_: (0,) * nd)

    grid_spec = pltpu.PrefetchScalarGridSpec(
        num_scalar_prefetch=2,
        grid=(b, n_steps),
        in_specs=[
            pl.BlockSpec((1, N_QMAPS, 128, tile), lambda i, t, qi, kj: (i, 0, 0, qi[t])),
            pl.BlockSpec((1, N_KBLOCKS, tile, 128), lambda i, t, qi, kj: (i, 0, kj[t], 0)),
            pl.BlockSpec((1, DIFF_HEADS, VD_ROWS, tile), lambda i, t, qi, kj: (i, 0, 0, kj[t])),
            pl.BlockSpec((1, FOX_HEADS, VF_ROWS, tile), lambda i, t, qi, kj: (i, 0, 0, kj[t])),
            const(km.shape), const(vdm.shape), const(vfm.shape),
            const(lq1.shape), const(lk1.shape), const(lq2.shape), const(lk2.shape),
            const(subg.shape),
        ],
        out_specs=pl.BlockSpec((1, tile, d_mix), lambda i, t, qi, kj: (i, qi[t], 0)),
        scratch_shapes=[pltpu.VMEM((N_QMAPS, 1, tile), F32),
                        pltpu.VMEM((2 * DIFF_HEADS, VD_ROWS, tile), F32),
                        pltpu.VMEM((FOX_HEADS, VF_ROWS, tile), F32)],
    )
    return pl.pallas_call(
        functools.partial(_attn_kernel, lambda_init=lambda_init),
        grid_spec=grid_spec,
        out_shape=jax.ShapeDtypeStruct((b, s, d_mix), BF16),
        compiler_params=pltpu.CompilerParams(dimension_semantics=("parallel", "arbitrary"),
                                             vmem_limit_bytes=VMEM_LIMIT),
        name="attn",
    )(jnp.asarray(qi_tab), jnp.asarray(kj_tab), qt, k, vd, vf, km, vdm, vfm,
      lq1, lk1, lq2, lk2, subg)


def _rope_swap_columns():
    n = 2 * DIFF_HEADS * DIFF_QK_DIM
    idx = np.arange(n)
    sgn = np.zeros(n, np.float32)
    r = idx % DIFF_QK_DIM
    half = ROPE_DIMS // 2
    first = r < half
    second = (r >= half) & (r < ROPE_DIMS)
    src = np.where(first, idx + half, np.where(second, idx - half, idx))
    sgn[first] = -1.0
    sgn[second] = 1.0
    return src.astype(np.int32), sgn


def _placement_matrix():
    m = np.zeros((3 * 128, FOX_HEADS * 128), np.float32)
    for p in range(3):
        for h in range(FOX_HEADS):
            m[p * 128 + h, h * 128 + FOX_HEAD_DIM + p] = 1.0
    return m


def kernel(x, meta_tokens, ffn1_norm_g, ffn1_w_gate_up, ffn1_w_down, mix_norm_g, w_in, b_forget, lam_q1, lam_k1, lam_q2, lam_k2, diff_subln_g, w_out, ffn2_norm_g, ffn2_w_gate_up, ffn2_w_down, final_norm_g):
    b, s, d = x.shape
    assert ffn1_norm_g.shape[0] == 1, "single-layer problem: meta rows are not carried past the mixer"
    assert s % ATTN_TILE == 0 and s % ROW_TILE == 0
    layer = 0
    lambda_init = _lambda_init(layer)

    row = lambda v: v.reshape(1, -1).astype(F32)
    wgu1 = ffn1_w_gate_up[layer].astype(BF16)
    wd1 = ffn1_w_down[layer].astype(BF16)
    wgu2 = ffn2_w_gate_up[layer].astype(BF16)
    wd2 = ffn2_w_down[layer].astype(BF16)
    wout = w_out[layer].astype(BF16)
    w = w_in[layer]
    n512 = 512
    dq_w, dk_w, dv_w, fq_w, fk_w, fv_w = (w[:, i * n512:(i + 1) * n512] for i in range(6))
    fl_w = w[:, 6 * n512:]
    src, sgn = _rope_swap_columns()
    dk_sw = jnp.take(dk_w, jnp.asarray(src), axis=1) * jnp.asarray(sgn)[None, :]
    fk_spread = jnp.pad(fk_w.reshape(d, FOX_HEADS, FOX_HEAD_DIM),
                        ((0, 0), (0, 0), (0, 128 - FOX_HEAD_DIM))).reshape(d, FOX_HEADS * 128)
    fl_blk = jnp.pad(fl_w, ((0, 0), (0, 128 - FOX_HEADS)))
    wrows = jnp.concatenate([dk_w, dk_sw, fk_spread, fl_blk], axis=1).astype(BF16)
    wt = jnp.concatenate([dq_w * QK_SCALE, fq_w * QK_SCALE, dv_w, fv_w], axis=1).T.astype(BF16)
    bfg = jnp.pad(b_forget[layer].reshape(1, -1).astype(F32), ((0, 0), (0, 128 - FOX_HEADS)))
    place = jnp.asarray(_placement_matrix()).astype(BF16)
    subg = diff_subln_g[layer].reshape(-1, 1).astype(F32)

    n_pos = max(N_META + s, META_TILE)
    pos = jnp.arange(n_pos, dtype=F32)
    inv_freq = jnp.power(ROPE_THETA, -jnp.arange(0, ROPE_DIMS, 2, dtype=F32) / ROPE_DIMS)
    ang = pos[:, None] * inv_freq[None, :]
    cos, sin = jnp.cos(ang), jnp.sin(ang)
    lane_r = np.arange(128) % DIFF_QK_DIM
    in_rope = jnp.asarray(lane_r < ROPE_DIMS)[None, :]
    lane_f = jnp.asarray(lane_r % (ROPE_DIMS // 2))
    cc_all = jnp.where(in_rope, jnp.take(cos, lane_f, axis=1), 1.0)
    ss_all = jnp.where(in_rope, jnp.take(sin, lane_f, axis=1), 0.0)
    cost_all, sint_all = cos.T, sin.T

    hm = jnp.pad(meta_tokens.astype(F32), ((0, META_TILE - N_META), (0, 0)))
    hm1 = _ffn1(hm, row(ffn1_norm_g[layer]), wgu1, wd1, META_TILE)
    zero_carry = jnp.zeros((8, 128), F32)
    _, km, vdm, vfm, carry_m = _proj(
        hm1[None], row(mix_norm_g[layer]), wrows, wt, bfg,
        cc_all[:META_TILE], ss_all[:META_TILE], cost_all[:, :META_TILE], sint_all[:, :META_TILE],
        zero_carry, place, META_TILE, N_META)

    h1 = _ffn1(x.reshape(b * s, d), row(ffn1_norm_g[layer]), wgu1, wd1, ROW_TILE)
    qt, k, vd, vf, _ = _proj(
        h1.reshape(b, s, d), row(mix_norm_g[layer]), wrows, wt, bfg,
        cc_all[N_META:N_META + s], ss_all[N_META:N_META + s],
        cost_all[:, N_META:N_META + s], sint_all[:, N_META:N_META + s],
        carry_m[0], place, ROW_TILE, ROW_TILE)
    o = _attn(qt, k, vd, vf, km[0], vdm[0], vfm[0],
              row(lam_q1[layer]), row(lam_k1[layer]), row(lam_q2[layer]), row(lam_k2[layer]),
              subg, lambda_init, ATTN_TILE)
    out = _tail(o.reshape(b * s, -1), h1, wout, row(ffn2_norm_g[layer]), wgu2, wd2,
                row(final_norm_g), ROW_TILE)
    return out.reshape(b, s, d)
```

```python
import functools
import math

import numpy as np
import jax
import jax.numpy as jnp
from jax import lax
from jax.experimental import pallas as pl
from jax.experimental.pallas import tpu as pltpu

F32 = jnp.float32
BF16 = jnp.bfloat16

N_META = 16
DIFF_HEADS = 4
DIFF_V_DIM = 128
DIFF_QK_DIM = 64
FOX_HEADS = 8
FOX_HEAD_DIM = 64
ROPE_DIMS = 16
ROPE_THETA = 500000.0
RMS_EPS = 1e-6
SUBLN_EPS = 1e-5
NEG_INF = -1e30
QK_SCALE = 0.125

N_QMAPS = 2 * DIFF_HEADS + FOX_HEADS
N_KBLOCKS = DIFF_HEADS + FOX_HEADS
BF16_SUBLANES = 16
VD_ROWS = DIFF_V_DIM + BF16_SUBLANES
VF_ROWS = FOX_HEAD_DIM + BF16_SUBLANES
META_TILE = 128
FFN_CHUNK = 256
ROW_TILE = 512
ATTN_TILE = 512
SCORE_LOOKAHEAD = 2
VMEM_LIMIT = 56 * 1024 * 1024


def _lambda_init(layer_idx):
    return 0.8 - 0.6 * math.exp(-0.3 * layer_idx)


def _rms(x, g, eps):
    ms = jnp.mean(x * x, axis=-1, keepdims=True)
    return x * lax.rsqrt(ms + eps) * g


def _split3(x):
    hi = x.astype(BF16)
    r = x - hi.astype(F32)
    mid = r.astype(BF16)
    lo = (r - mid.astype(F32)).astype(BF16)
    return hi, mid, lo


def _const_spec(shape):
    nd = len(shape)
    return pl.BlockSpec(shape, lambda *_: (0,) * nd, pipeline_mode=pl.Buffered(1))


def _swiglu_half_step(h, g_ref, wgu_ref, wd_ref):
    d_ff = wd_ref.shape[0]
    hn = _rms(h, g_ref[...], RMS_EPS).astype(BF16)
    acc = jnp.zeros(h.shape, F32)
    for c in range(d_ff // FFN_CHUNK):
        lo, hi = c * FFN_CHUNK, (c + 1) * FFN_CHUNK
        gate = jnp.dot(hn, wgu_ref[:, lo:hi], preferred_element_type=F32)
        up = jnp.dot(hn, wgu_ref[:, d_ff + lo:d_ff + hi], preferred_element_type=F32)
        act = (gate / (1.0 + jnp.exp(-gate)) * up).astype(BF16)
        acc = acc + jnp.dot(act, wd_ref[lo:hi, :], preferred_element_type=F32)
    return h + 0.5 * acc


def _ffn1_kernel(h_ref, g_ref, wgu_ref, wd_ref, o_ref):
    o_ref[...] = _swiglu_half_step(h_ref[...], g_ref, wgu_ref, wd_ref)


def _ffn1(h, g, wgu, wd, tm):
    rows, d = h.shape
    return pl.pallas_call(
        _ffn1_kernel,
        grid=(rows // tm,),
        in_specs=[pl.BlockSpec((tm, d), lambda i: (i, 0)),
                  _const_spec(g.shape), _const_spec(wgu.shape), _const_spec(wd.shape)],
        out_specs=pl.BlockSpec((tm, d), lambda i: (i, 0)),
        out_shape=jax.ShapeDtypeStruct((rows, d), F32),
        compiler_params=pltpu.CompilerParams(dimension_semantics=("parallel",),
                                             vmem_limit_bytes=VMEM_LIMIT),
        name="ffn1",
    )(h, g, wgu, wd)


def _tail_kernel(o_ref, h_ref, wout_ref, g_ref, wgu_ref, wd_ref, gf_ref, out_ref):
    h2 = h_ref[...] + jnp.dot(o_ref[...], wout_ref[...], preferred_element_type=F32)
    h3 = _swiglu_half_step(h2, g_ref, wgu_ref, wd_ref)
    out_ref[...] = _rms(h3, gf_ref[...], RMS_EPS)


def _tail(o, h1, wout, g, wgu, wd, gf, tm):
    rows, d = h1.shape
    return pl.pallas_call(
        _tail_kernel,
        grid=(rows // tm,),
        in_specs=[pl.BlockSpec((tm, d), lambda i: (i, 0)),
                  pl.BlockSpec((tm, d), lambda i: (i, 0)),
                  _const_spec(wout.shape), _const_spec(g.shape), _const_spec(wgu.shape),
                  _const_spec(wd.shape), _const_spec(gf.shape)],
        out_specs=pl.BlockSpec((tm, d), lambda i: (i, 0)),
        out_shape=jax.ShapeDtypeStruct((rows, d), F32),
        compiler_params=pltpu.CompilerParams(dimension_semantics=("parallel",),
                                             vmem_limit_bytes=VMEM_LIMIT),
        name="tail",
    )(o, h1, wout, g, wgu, wd, gf)


def _proj_kernel(h_ref, g_ref, wrows_ref, wt_ref, bf_ref, cc_ref, ss_ref, cost_ref, sint_ref,
                 carry0_ref, place_ref,
                 qt_ref, k_ref, vd_ref, vf_ref, carry_out_ref, carry_scr, *, n_valid):
    tm = h_ref.shape[1]

    @pl.when(pl.program_id(1) == 0)
    def _():
        carry_scr[...] = carry0_ref[...]

    hn = _rms(h_ref[0], g_ref[...], RMS_EPS).astype(BF16)
    kr = jnp.dot(hn, wrows_ref[...], preferred_element_type=F32)
    pt = lax.dot_general(wt_ref[...], hn, (((1,), (1,)), ((), ())),
                         preferred_element_type=F32)

    cc = cc_ref[...]
    ss = ss_ref[...]
    n_dk = DIFF_HEADS * 128
    for p in range(DIFF_HEADS):
        kd = kr[:, p * 128:(p + 1) * 128]
        ks = kr[:, n_dk + p * 128:n_dk + (p + 1) * 128]
        k_ref[0, p] = (kd * cc + ks * ss).astype(BF16)

    fk0 = 2 * n_dk
    fl0 = fk0 + FOX_HEADS * 128
    fl = kr[:, fl0:fl0 + 128] + bf_ref[...]
    logf = jnp.minimum(fl, 0.0) - jnp.log(1.0 + jnp.exp(-jnp.abs(fl)))
    lane = lax.broadcasted_iota(jnp.int32, (tm, 128), 1)
    keep = lane < FOX_HEADS
    if n_valid < tm:
        row = lax.broadcasted_iota(jnp.int32, (tm, 128), 0)
        keep = jnp.logical_and(keep, row < n_valid)
    logf = jnp.where(keep, logf, 0.0)
    parts = jnp.concatenate(_split3(logf), axis=1)
    r_i = lax.broadcasted_iota(jnp.int32, (tm, tm), 0)
    c_i = lax.broadcasted_iota(jnp.int32, (tm, tm), 1)
    tri = jnp.where(r_i >= c_i, 1.0, 0.0).astype(BF16)
    cs = jnp.dot(tri, parts, preferred_element_type=F32)
    cum = cs[:, 0:128] + cs[:, 128:256] + cs[:, 256:384] + carry_scr[0:1, :]
    carry_scr[...] = jnp.broadcast_to(cum[tm - 1:tm, :], carry_scr.shape)
    carry_out_ref[0] = carry_scr[...]
    cparts = jnp.concatenate(_split3(cum), axis=1)
    aug = jnp.dot(cparts, place_ref[...], preferred_element_type=F32)
    for hh in range(FOX_HEADS):
        kf = kr[:, fk0 + hh * 128:fk0 + (hh + 1) * 128] + aug[:, hh * 128:(hh + 1) * 128]
        k_ref[0, DIFF_HEADS + hh] = kf.astype(BF16)

    cost = cost_ref[...]
    sint = sint_ref[...]
    half = ROPE_DIMS // 2
    zeros64 = jnp.zeros((64, tm), BF16)
    for h in range(DIFF_HEADS):
        for c in range(2):
            base = h * 128 + c * 64
            x1 = pt[base:base + half]
            x2 = pt[base + half:base + ROPE_DIMS]
            blk = jnp.concatenate([x1 * cost - x2 * sint, x2 * cost + x1 * sint,
                                   pt[base + ROPE_DIMS:base + 64]], axis=0).astype(BF16)
            u = 2 * h + c
            qt_ref[0, u, c * 64:(c + 1) * 64, :] = blk
            qt_ref[0, u, (1 - c) * 64:(2 - c) * 64, :] = zeros64
    fq0 = DIFF_HEADS * 128
    rows64 = lax.broadcasted_iota(jnp.int32, (64, tm), 0)
    neg_rows = jnp.where(rows64 < 3, -1.0, 0.0).astype(BF16)
    for hh in range(FOX_HEADS):
        u = 2 * DIFF_HEADS + hh
        qt_ref[0, u, 0:64, :] = pt[fq0 + hh * 64:fq0 + (hh + 1) * 64].astype(BF16)
        qt_ref[0, u, 64:128, :] = neg_rows

    ones_rows = jnp.ones((BF16_SUBLANES, tm), BF16)
    dv0 = fq0 + FOX_HEADS * 64
    for h in range(DIFF_HEADS):
        vd_ref[0, h, 0:DIFF_V_DIM, :] = pt[dv0 + h * 128:dv0 + (h + 1) * 128].astype(BF16)
        vd_ref[0, h, DIFF_V_DIM:VD_ROWS, :] = ones_rows
    fv0 = dv0 + DIFF_HEADS * 128
    for hh in range(FOX_HEADS):
        vf_ref[0, hh, 0:FOX_HEAD_DIM, :] = pt[fv0 + hh * 64:fv0 + (hh + 1) * 64].astype(BF16)
        vf_ref[0, hh, FOX_HEAD_DIM:VF_ROWS, :] = ones_rows


def _proj(h, g, wrows, wt, bfg, cc, ss, cost, sint, carry0, place, tm, n_valid):
    b, s, d = h.shape
    nt = s // tm
    out_shape = (
        jax.ShapeDtypeStruct((b, N_QMAPS, 128, s), BF16),
        jax.ShapeDtypeStruct((b, N_KBLOCKS, s, 128), BF16),
        jax.ShapeDtypeStruct((b, DIFF_HEADS, VD_ROWS, s), BF16),
        jax.ShapeDtypeStruct((b, FOX_HEADS, VF_ROWS, s), BF16),
        jax.ShapeDtypeStruct((b, 8, 128), F32),
    )
    return pl.pallas_call(
        functools.partial(_proj_kernel, n_valid=n_valid),
        grid=(b, nt),
        in_specs=[pl.BlockSpec((1, tm, d), lambda i, t: (i, t, 0)),
                  _const_spec(g.shape), _const_spec(wrows.shape), _const_spec(wt.shape),
                  _const_spec(bfg.shape),
                  pl.BlockSpec((tm, 128), lambda i, t: (t, 0)),
                  pl.BlockSpec((tm, 128), lambda i, t: (t, 0)),
                  pl.BlockSpec((8, tm), lambda i, t: (0, t)),
                  pl.BlockSpec((8, tm), lambda i, t: (0, t)),
                  _const_spec(carry0.shape), _const_spec(place.shape)],
        out_specs=(pl.BlockSpec((1, N_QMAPS, 128, tm), lambda i, t: (i, 0, 0, t)),
                   pl.BlockSpec((1, N_KBLOCKS, tm, 128), lambda i, t: (i, 0, t, 0)),
                   pl.BlockSpec((1, DIFF_HEADS, VD_ROWS, tm), lambda i, t: (i, 0, 0, t)),
                   pl.BlockSpec((1, FOX_HEADS, VF_ROWS, tm), lambda i, t: (i, 0, 0, t)),
                   pl.BlockSpec((1, 8, 128), lambda i, t: (i, 0, 0))),
        out_shape=out_shape,
        scratch_shapes=[pltpu.VMEM((8, 128), F32)],
        compiler_params=pltpu.CompilerParams(dimension_semantics=("parallel", "arbitrary"),
                                             vmem_limit_bytes=VMEM_LIMIT),
        name="proj",
    )(h, g, wrows, wt, bfg, cc, ss, cost, sint, carry0, place)


def _attn_kernel(qi_ref, kj_ref,
                 qt_ref, k_ref, vd_ref, vf_ref, km_ref, vdm_ref, vfm_ref,
                 lq1_ref, lk1_ref, lq2_ref, lk2_ref, subg_ref,
                 o_ref, m_scr, accd_scr, accf_scr, *, lambda_init):
    s_idx = pl.program_id(1)
    qi = qi_ref[s_idx]
    kj = kj_ref[s_idx]
    tq = qt_ref.shape[3]
    tk = k_ref.shape[2]

    n_dmaps = 2 * DIFF_HEADS

    def sweep(k_blocks, vd_blocks, vf_blocks, mask):
        def scores(u):
            kb = u // 2 if u < n_dmaps else u - DIFF_HEADS
            return jnp.dot(k_blocks(kb), qt_ref[0, u], preferred_element_type=F32)

        def update(u, st):
            if u < n_dmaps:
                acc_scr, a, v_t = accd_scr, u, vd_blocks(u // 2)
            else:
                acc_scr, a, v_t = accf_scr, u - n_dmaps, vf_blocks(u - n_dmaps)
            if mask is not None:
                st = jnp.where(mask, st, NEG_INF)
            m_old = m_scr[u]
            m_new = jnp.maximum(m_old, jnp.max(st, axis=0, keepdims=True))
            alpha = jnp.exp(m_old - m_new)
            p = jnp.exp(st - m_new).astype(BF16)
            pv = jnp.dot(v_t, p, preferred_element_type=F32)
            acc_scr[a] = acc_scr[a] * alpha + pv
            m_scr[u] = m_new

        pending = {u: scores(u) for u in range(SCORE_LOOKAHEAD)}
        for u in range(N_QMAPS):
            if u + SCORE_LOOKAHEAD < N_QMAPS:
                pending[u + SCORE_LOOKAHEAD] = scores(u + SCORE_LOOKAHEAD)
            update(u, pending.pop(u))

    @pl.when(kj == 0)
    def _():
        m_scr[...] = jnp.full(m_scr.shape, NEG_INF, F32)
        accd_scr[...] = jnp.zeros(accd_scr.shape, F32)
        accf_scr[...] = jnp.zeros(accf_scr.shape, F32)
        key = lax.broadcasted_iota(jnp.int32, (META_TILE, tq), 0)
        sweep(lambda p: km_ref[p], lambda h: vdm_ref[h], lambda h: vfm_ref[h], key < N_META)

    @pl.when(kj < qi)
    def _():
        sweep(lambda p: k_ref[0, p], lambda h: vd_ref[0, h], lambda h: vf_ref[0, h], None)

    @pl.when(kj == qi)
    def _():
        key = lax.broadcasted_iota(jnp.int32, (tk, tq), 0)
        qry = lax.broadcasted_iota(jnp.int32, (tk, tq), 1)
        sweep(lambda p: k_ref[0, p], lambda h: vd_ref[0, h], lambda h: vf_ref[0, h], key <= qry)

        lam = (jnp.exp(jnp.sum(lq1_ref[...] * lk1_ref[...], axis=1, keepdims=True))
               - jnp.exp(jnp.sum(lq2_ref[...] * lk2_ref[...], axis=1, keepdims=True))
               + lambda_init)
        for h in range(DIFF_HEADS):
            a1 = accd_scr[2 * h]
            a2 = accd_scr[2 * h + 1]
            o = (a1[0:DIFF_V_DIM] / a1[DIFF_V_DIM:DIFF_V_DIM + 1]
                 - lam * (a2[0:DIFF_V_DIM] / a2[DIFF_V_DIM:DIFF_V_DIM + 1]))
            ms = jnp.mean(o * o, axis=0, keepdims=True)
            o = o * lax.rsqrt(ms + SUBLN_EPS) * subg_ref[...] * (1.0 - lambda_init)
            o_ref[0, :, h * 128:(h + 1) * 128] = o.T.astype(o_ref.dtype)
        for i in range(FOX_HEADS // 2):
            a0 = accf_scr[2 * i]
            a1 = accf_scr[2 * i + 1]
            o = jnp.concatenate(
                [a0[0:FOX_HEAD_DIM] / a0[FOX_HEAD_DIM:FOX_HEAD_DIM + 1],
                 a1[0:FOX_HEAD_DIM] / a1[FOX_HEAD_DIM:FOX_HEAD_DIM + 1]], axis=0)
            c0 = DIFF_HEADS * DIFF_V_DIM + i * 128
            o_ref[0, :, c0:c0 + 128] = o.T.astype(o_ref.dtype)


def _attn(qt, k, vd, vf, km, vdm, vfm, lq1, lk1, lq2, lk2, subg, lambda_init, tile):
    b, _, _, s = qt.shape
    nq = s // tile
    qi_tab = np.concatenate([np.full(i + 1, i, np.int32) for i in range(nq)])
    kj_tab = np.concatenate([np.arange(i + 1, dtype=np.int32) for i in range(nq)])
    n_steps = int(qi_tab.shape[0])
    d_mix = DIFF_HEADS * DIFF_V_DIM + FOX_HEADS * FOX_HEAD_DIM

    def const(shape):
        nd = len(shape)
        return pl.BlockSpec(shape, lambda *_: (0,) * nd)

    grid_spec = pltpu.PrefetchScalarGridSpec(
        num_scalar_prefetch=2,
        grid=(b, n_steps),
        in_specs=[
            pl.BlockSpec((1, N_QMAPS, 128, tile), lambda i, t, qi, kj: (i, 0, 0, qi[t])),
            pl.BlockSpec((1, N_KBLOCKS, tile, 128), lambda i, t, qi, kj: (i, 0, kj[t], 0)),
            pl.BlockSpec((1, DIFF_HEADS, VD_ROWS, tile), lambda i, t, qi, kj: (i, 0, 0, kj[t])),
            pl.BlockSpec((1, FOX_HEADS, VF_ROWS, tile), lambda i, t, qi, kj: (i, 0, 0, kj[t])),
            const(km.shape), const(vdm.shape), const(vfm.shape),
            const(lq1.shape), const(lk1.shape), const(lq2.shape), const(lk2.shape),
            const(subg.shape),
        ],
        out_specs=pl.BlockSpec((1, tile, d_mix), lambda i, t, qi, kj: (i, qi[t], 0)),
        scratch_shapes=[pltpu.VMEM((N_QMAPS, 1, tile), F32),
                        pltpu.VMEM((2 * DIFF_HEADS, VD_ROWS, tile), F32),
                        pltpu.VMEM((FOX_HEADS, VF_ROWS, tile), F32)],
    )
    return pl.pallas_call(
        functools.partial(_attn_kernel, lambda_init=lambda_init),
        grid_spec=grid_spec,
        out_shape=jax.ShapeDtypeStruct((b, s, d_mix), BF16),
        compiler_params=pltpu.CompilerParams(dimension_semantics=("parallel", "arbitrary"),
                                             vmem_limit_bytes=VMEM_LIMIT),
        name="attn",
    )(jnp.asarray(qi_tab), jnp.asarray(kj_tab), qt, k, vd, vf, km, vdm, vfm,
      lq1, lk1, lq2, lk2, subg)


def _rope_swap_columns():
    n = 2 * DIFF_HEADS * DIFF_QK_DIM
    idx = np.arange(n)
    sgn = np.zeros(n, np.float32)
    r = idx % DIFF_QK_DIM
    half = ROPE_DIMS // 2
    first = r < half
    second = (r >= half) & (r < ROPE_DIMS)
    src = np.where(first, idx + half, np.where(second, idx - half, idx))
    sgn[first] = -1.0
    sgn[second] = 1.0
    return src.astype(np.int32), sgn


def _placement_matrix():
    m = np.zeros((3 * 128, FOX_HEADS * 128), np.float32)
    for p in range(3):
        for h in range(FOX_HEADS):
            m[p * 128 + h, h * 128 + FOX_HEAD_DIM + p] = 1.0
    return m


def kernel(x, meta_tokens, ffn1_norm_g, ffn1_w_gate_up, ffn1_w_down, mix_norm_g, w_in, b_forget, lam_q1, lam_k1, lam_q2, lam_k2, diff_subln_g, w_out, ffn2_norm_g, ffn2_w_gate_up, ffn2_w_down, final_norm_g):
    b, s, d = x.shape
    assert ffn1_norm_g.shape[0] == 1, "single-layer problem: meta rows are not carried past the mixer"
    assert s % ATTN_TILE == 0 and s % ROW_TILE == 0
    layer = 0
    lambda_init = _lambda_init(layer)

    row = lambda v: v.reshape(1, -1).astype(F32)
    wgu1 = ffn1_w_gate_up[layer].astype(BF16)
    wd1 = ffn1_w_down[layer].astype(BF16)
    wgu2 = ffn2_w_gate_up[layer].astype(BF16)
    wd2 = ffn2_w_down[layer].astype(BF16)
    wout = w_out[layer].astype(BF16)
    w = w_in[layer]
    n512 = 512
    dq_w, dk_w, dv_w, fq_w, fk_w, fv_w = (w[:, i * n512:(i + 1) * n512] for i in range(6))
    fl_w = w[:, 6 * n512:]
    src, sgn = _rope_swap_columns()
    dk_sw = jnp.take(dk_w, jnp.asarray(src), axis=1) * jnp.asarray(sgn)[None, :]
    fk_spread = jnp.pad(fk_w.reshape(d, FOX_HEADS, FOX_HEAD_DIM),
                        ((0, 0), (0, 0), (0, 128 - FOX_HEAD_DIM))).reshape(d, FOX_HEADS * 128)
    fl_blk = jnp.pad(fl_w, ((0, 0), (0, 128 - FOX_HEADS)))
    wrows = jnp.concatenate([dk_w, dk_sw, fk_spread, fl_blk], axis=1).astype(BF16)
    wt = jnp.concatenate([dq_w * QK_SCALE, fq_w * QK_SCALE, dv_w, fv_w], axis=1).T.astype(BF16)
    bfg = jnp.pad(b_forget[layer].reshape(1, -1).astype(F32), ((0, 0), (0, 128 - FOX_HEADS)))
    place = jnp.asarray(_placement_matrix()).astype(BF16)
    subg = diff_subln_g[layer].reshape(-1, 1).astype(F32)

    n_pos = max(N_META + s, META_TILE)
    pos = jnp.arange(n_pos, dtype=F32)
    inv_freq = jnp.power(ROPE_THETA, -jnp.arange(0, ROPE_DIMS, 2, dtype=F32) / ROPE_DIMS)
    ang = pos[:, None] * inv_freq[None, :]
    cos, sin = jnp.cos(ang), jnp.sin(ang)
    lane_r = np.arange(128) % DIFF_QK_DIM
    in_rope = jnp.asarray(lane_r < ROPE_DIMS)[None, :]
    lane_f = jnp.asarray(lane_r % (ROPE_DIMS // 2))
    cc_all = jnp.where(in_rope, jnp.take(cos, lane_f, axis=1), 1.0)
    ss_all = jnp.where(in_rope, jnp.take(sin, lane_f, axis=1), 0.0)
    cost_all, sint_all = cos.T, sin.T

    hm = jnp.pad(meta_tokens.astype(F32), ((0, META_TILE - N_META), (0, 0)))
    hm1 = _ffn1(hm, row(ffn1_norm_g[layer]), wgu1, wd1, META_TILE)
    zero_carry = jnp.zeros((8, 128), F32)
    _, km, vdm, vfm, carry_m = _proj(
        hm1[None], row(mix_norm_g[layer]), wrows, wt, bfg,
        cc_all[:META_TILE], ss_all[:META_TILE], cost_all[:, :META_TILE], sint_all[:, :META_TILE],
        zero_carry, place, META_TILE, N_META)

    h1 = _ffn1(x.reshape(b * s, d), row(ffn1_norm_g[layer]), wgu1, wd1, ROW_TILE)
    qt, k, vd, vf, _ = _proj(
        h1.reshape(b, s, d), row(mix_norm_g[layer]), wrows, wt, bfg,
        cc_all[N_META:N_META + s], ss_all[N_META:N_META + s],
        cost_all[:, N_META:N_META + s], sint_all[:, N_META:N_META + s],
        carry_m[0], place, ROW_TILE, ROW_TILE)
    o = _attn(qt, k, vd, vf, km[0], vdm[0], vfm[0],
              row(lam_q1[layer]), row(lam_k1[layer]), row(lam_q2[layer]), row(lam_k2[layer]),
              subg, lambda_init, ATTN_TILE)
    out = _tail(o.reshape(b * s, -1), h1, wout, row(ffn2_norm_g[layer]), wgu2, wd2,
                row(final_norm_g), ROW_TILE)
    return out.reshape(b, s, d)
```

```python
import functools
import math

import numpy as np
import jax
import jax.numpy as jnp
from jax import lax
from jax.experimental import pallas as pl
from jax.experimental.pallas import tpu as pltpu

F32 = jnp.float32
BF16 = jnp.bfloat16

N_META = 16
DIFF_HEADS = 4
DIFF_V_DIM = 128
DIFF_QK_DIM = 64
FOX_HEADS = 8
FOX_HEAD_DIM = 64
ROPE_DIMS = 16
ROPE_THETA = 500000.0
RMS_EPS = 1e-6
SUBLN_EPS = 1e-5
NEG_INF = -1e30
QK_SCALE = 0.125
LOG2E = math.log2(math.e)

N_QMAPS = 2 * DIFF_HEADS + FOX_HEADS
N_KBLOCKS = DIFF_HEADS + FOX_HEADS
BF16_SUBLANES = 16
VD_ROWS = DIFF_V_DIM + BF16_SUBLANES
VF_ROWS = FOX_HEAD_DIM + BF16_SUBLANES
META_TILE = 128
FFN_CHUNK = 256
ROW_TILE = 512
ATTN_TILE = 512
QCOLS = 256
SCORE_LOOKAHEAD = 4
VMEM_LIMIT = 56 * 1024 * 1024


def _lambda_init(layer_idx):
    return 0.8 - 0.6 * math.exp(-0.3 * layer_idx)


def _rms(x, g, eps):
    ms = jnp.mean(x * x, axis=-1, keepdims=True)
    return x * lax.rsqrt(ms + eps) * g


def _split3(x):
    hi = x.astype(BF16)
    r = x - hi.astype(F32)
    mid = r.astype(BF16)
    lo = (r - mid.astype(F32)).astype(BF16)
    return hi, mid, lo


def _const_spec(shape):
    nd = len(shape)
    return pl.BlockSpec(shape, lambda *_: (0,) * nd, pipeline_mode=pl.Buffered(1))


def _swiglu_half_step(h, g_ref, wgu_ref, wd_ref):
    d_ff = wd_ref.shape[0]
    hn = _rms(h, g_ref[...], RMS_EPS).astype(BF16)
    acc = jnp.zeros(h.shape, F32)
    for c in range(d_ff // FFN_CHUNK):
        lo, hi = c * FFN_CHUNK, (c + 1) * FFN_CHUNK
        gate = jnp.dot(hn, wgu_ref[:, lo:hi], preferred_element_type=F32)
        up = jnp.dot(hn, wgu_ref[:, d_ff + lo:d_ff + hi], preferred_element_type=F32)
        act = (gate / (1.0 + jnp.exp(-gate)) * up).astype(BF16)
        acc = acc + jnp.dot(act, wd_ref[lo:hi, :], preferred_element_type=F32)
    return h + 0.5 * acc


def _ffn1_kernel(h_ref, g_ref, wgu_ref, wd_ref, o_ref):
    o_ref[...] = _swiglu_half_step(h_ref[...], g_ref, wgu_ref, wd_ref)


def _ffn1(h, g, wgu, wd, tm):
    rows, d = h.shape
    return pl.pallas_call(
        _ffn1_kernel,
        grid=(rows // tm,),
        in_specs=[pl.BlockSpec((tm, d), lambda i: (i, 0)),
                  _const_spec(g.shape), _const_spec(wgu.shape), _const_spec(wd.shape)],
        out_specs=pl.BlockSpec((tm, d), lambda i: (i, 0)),
        out_shape=jax.ShapeDtypeStruct((rows, d), F32),
        compiler_params=pltpu.CompilerParams(dimension_semantics=("parallel",),
                                             vmem_limit_bytes=VMEM_LIMIT),
        name="ffn1",
    )(h, g, wgu, wd)


def _tail_kernel(o_ref, h_ref, wout_ref, g_ref, wgu_ref, wd_ref, gf_ref, out_ref):
    h2 = h_ref[...] + jnp.dot(o_ref[...], wout_ref[...], preferred_element_type=F32)
    h3 = _swiglu_half_step(h2, g_ref, wgu_ref, wd_ref)
    out_ref[...] = _rms(h3, gf_ref[...], RMS_EPS)


def _tail(o, h1, wout, g, wgu, wd, gf, tm):
    rows, d = h1.shape
    return pl.pallas_call(
        _tail_kernel,
        grid=(rows // tm,),
        in_specs=[pl.BlockSpec((tm, d), lambda i: (i, 0)),
                  pl.BlockSpec((tm, d), lambda i: (i, 0)),
                  _const_spec(wout.shape), _const_spec(g.shape), _const_spec(wgu.shape),
                  _const_spec(wd.shape), _const_spec(gf.shape)],
        out_specs=pl.BlockSpec((tm, d), lambda i: (i, 0)),
        out_shape=jax.ShapeDtypeStruct((rows, d), F32),
        compiler_params=pltpu.CompilerParams(dimension_semantics=("parallel",),
                                             vmem_limit_bytes=VMEM_LIMIT),
        name="tail",
    )(o, h1, wout, g, wgu, wd, gf)


def _proj_kernel(h_ref, g_ref, wrows_ref, wt_ref, bf_ref, cc_ref, ss_ref, cost_ref, sint_ref,
                 carry0_ref, place_ref,
                 qt_ref, k_ref, vd_ref, vf_ref, carry_out_ref, carry_scr, *, n_valid):
    tm = h_ref.shape[1]

    @pl.when(pl.program_id(1) == 0)
    def _():
        carry_scr[...] = carry0_ref[...]

    hn = _rms(h_ref[0], g_ref[...], RMS_EPS).astype(BF16)
    kr = jnp.dot(hn, wrows_ref[...], preferred_element_type=F32)
    pt = lax.dot_general(wt_ref[...], hn, (((1,), (1,)), ((), ())),
                         preferred_element_type=F32)

    cc = cc_ref[...]
    ss = ss_ref[...]
    n_dk = DIFF_HEADS * 128
    for p in range(DIFF_HEADS):
        kd = kr[:, p * 128:(p + 1) * 128]
        ks = kr[:, n_dk + p * 128:n_dk + (p + 1) * 128]
        k_ref[0, p] = (kd * cc + ks * ss).astype(BF16)

    fk0 = 2 * n_dk
    fl0 = fk0 + FOX_HEADS * 128
    fl = kr[:, fl0:fl0 + 128] + bf_ref[...]
    logf = jnp.minimum(fl, 0.0) - jnp.log(1.0 + jnp.exp(-jnp.abs(fl)))
    lane = lax.broadcasted_iota(jnp.int32, (tm, 128), 1)
    keep = lane < FOX_HEADS
    if n_valid < tm:
        row = lax.broadcasted_iota(jnp.int32, (tm, 128), 0)
        keep = jnp.logical_and(keep, row < n_valid)
    logf = jnp.where(keep, logf, 0.0)
    parts = jnp.concatenate(_split3(logf), axis=1)
    r_i = lax.broadcasted_iota(jnp.int32, (tm, tm), 0)
    c_i = lax.broadcasted_iota(jnp.int32, (tm, tm), 1)
    tri = jnp.where(r_i >= c_i, 1.0, 0.0).astype(BF16)
    cs = jnp.dot(tri, parts, preferred_element_type=F32)
    cum = cs[:, 0:128] + cs[:, 128:256] + cs[:, 256:384] + carry_scr[0:1, :]
    carry_scr[...] = jnp.broadcast_to(cum[tm - 1:tm, :], carry_scr.shape)
    carry_out_ref[0] = carry_scr[...]
    cparts = jnp.concatenate(_split3(cum * LOG2E), axis=1)
    aug = jnp.dot(cparts, place_ref[...], preferred_element_type=F32)
    for hh in range(FOX_HEADS):
        kf = kr[:, fk0 + hh * 128:fk0 + (hh + 1) * 128] + aug[:, hh * 128:(hh + 1) * 128]
        k_ref[0, DIFF_HEADS + hh] = kf.astype(BF16)

    cost = cost_ref[...]
    sint = sint_ref[...]
    half = ROPE_DIMS // 2
    zeros64 = jnp.zeros((64, tm), BF16)
    for h in range(DIFF_HEADS):
        for c in range(2):
            base = h * 128 + c * 64
            x1 = pt[base:base + half]
            x2 = pt[base + half:base + ROPE_DIMS]
            blk = jnp.concatenate([x1 * cost - x2 * sint, x2 * cost + x1 * sint,
                                   pt[base + ROPE_DIMS:base + 64]], axis=0).astype(BF16)
            u = 2 * h + c
            qt_ref[0, 0, u, c * 64:(c + 1) * 64, :] = blk
            qt_ref[0, 0, u, (1 - c) * 64:(2 - c) * 64, :] = zeros64
    fq0 = DIFF_HEADS * 128
    rows64 = lax.broadcasted_iota(jnp.int32, (64, tm), 0)
    neg_rows = jnp.where(rows64 < 3, -1.0, 0.0).astype(BF16)
    for hh in range(FOX_HEADS):
        u = 2 * DIFF_HEADS + hh
        qt_ref[0, 0, u, 0:64, :] = pt[fq0 + hh * 64:fq0 + (hh + 1) * 64].astype(BF16)
        qt_ref[0, 0, u, 64:128, :] = neg_rows

    ones_rows = jnp.ones((BF16_SUBLANES, tm), BF16)
    dv0 = fq0 + FOX_HEADS * 64
    for h in range(DIFF_HEADS):
        vd_ref[0, 0, h, 0:DIFF_V_DIM, :] = pt[dv0 + h * 128:dv0 + (h + 1) * 128].astype(BF16)
        vd_ref[0, 0, h, DIFF_V_DIM:VD_ROWS, :] = ones_rows
    fv0 = dv0 + DIFF_HEADS * 128
    for hh in range(FOX_HEADS):
        vf_ref[0, 0, hh, 0:FOX_HEAD_DIM, :] = pt[fv0 + hh * 64:fv0 + (hh + 1) * 64].astype(BF16)
        vf_ref[0, 0, hh, FOX_HEAD_DIM:VF_ROWS, :] = ones_rows


def _proj(h, g, wrows, wt, bfg, cc, ss, cost, sint, carry0, place, tm, n_valid):
    b, s, d = h.shape
    nt = s // tm
    out_shape = (
        jax.ShapeDtypeStruct((b, nt, N_QMAPS, 128, tm), BF16),
        jax.ShapeDtypeStruct((b, N_KBLOCKS, s, 128), BF16),
        jax.ShapeDtypeStruct((b, nt, DIFF_HEADS, VD_ROWS, tm), BF16),
        jax.ShapeDtypeStruct((b, nt, FOX_HEADS, VF_ROWS, tm), BF16),
        jax.ShapeDtypeStruct((b, 8, 128), F32),
    )
    return pl.pallas_call(
        functools.partial(_proj_kernel, n_valid=n_valid),
        grid=(b, nt),
        in_specs=[pl.BlockSpec((1, tm, d), lambda i, t: (i, t, 0)),
                  _const_spec(g.shape), _const_spec(wrows.shape), _const_spec(wt.shape),
                  _const_spec(bfg.shape),
                  pl.BlockSpec((tm, 128), lambda i, t: (t, 0)),
                  pl.BlockSpec((tm, 128), lambda i, t: (t, 0)),
                  pl.BlockSpec((8, tm), lambda i, t: (0, t)),
                  pl.BlockSpec((8, tm), lambda i, t: (0, t)),
                  _const_spec(carry0.shape), _const_spec(place.shape)],
        out_specs=(pl.BlockSpec((1, 1, N_QMAPS, 128, tm), lambda i, t: (i, t, 0, 0, 0)),
                   pl.BlockSpec((1, N_KBLOCKS, tm, 128), lambda i, t: (i, 0, t, 0)),
                   pl.BlockSpec((1, 1, DIFF_HEADS, VD_ROWS, tm), lambda i, t: (i, t, 0, 0, 0)),
                   pl.BlockSpec((1, 1, FOX_HEADS, VF_ROWS, tm), lambda i, t: (i, t, 0, 0, 0)),
                   pl.BlockSpec((1, 8, 128), lambda i, t: (i, 0, 0))),
        out_shape=out_shape,
        scratch_shapes=[pltpu.VMEM((8, 128), F32)],
        compiler_params=pltpu.CompilerParams(dimension_semantics=("parallel", "arbitrary"),
                                             vmem_limit_bytes=VMEM_LIMIT),
        name="proj",
    )(h, g, wrows, wt, bfg, cc, ss, cost, sint, carry0, place)


def _attn_kernel(qi_ref, kj_ref,
                 qt_ref, k_ref, vd_ref, vf_ref, km_ref, vdm_ref, vfm_ref,
                 lq1_ref, lk1_ref, lq2_ref, lk2_ref, subg_ref,
                 o_ref, m_scr, accd_scr, accf_scr, *, lambda_init):
    s_idx = pl.program_id(1)
    qi = qi_ref[s_idx]
    kj = kj_ref[s_idx]
    tq = qt_ref.shape[4]
    tk = k_ref.shape[2]

    n_dmaps = 2 * DIFF_HEADS
    n_cb = tq // QCOLS
    units = [(u, cb) for u in range(N_QMAPS) for cb in range(n_cb)]

    def sweep(k_rows, vd_cols, vf_cols, n_keys, mask):
        def scores(unit):
            u, cb = unit
            kb = u // 2 if u < n_dmaps else u - DIFF_HEADS
            return jnp.dot(k_rows(kb, n_keys(cb)),
                           qt_ref[0, 0, u, :, cb * QCOLS:(cb + 1) * QCOLS],
                           preferred_element_type=F32)

        def update(unit, st):
            u, cb = unit
            cols = slice(cb * QCOLS, (cb + 1) * QCOLS)
            if u < n_dmaps:
                acc_scr, a, v_t = accd_scr, u, vd_cols(u // 2, n_keys(cb))
            else:
                acc_scr, a, v_t = accf_scr, u - n_dmaps, vf_cols(u - n_dmaps, n_keys(cb))
            if mask is not None:
                st = jnp.where(mask(cb), st, NEG_INF)
            m_old = m_scr[u * n_cb + cb]
            m_new = jnp.maximum(m_old, jnp.max(st, axis=0, keepdims=True))
            alpha = jnp.exp2(m_old - m_new)
            p = jnp.exp2(st - m_new).astype(BF16)
            pv = jnp.dot(v_t, p, preferred_element_type=F32)
            acc_scr[a, :, cols] = acc_scr[a, :, cols] * alpha + pv
            m_scr[u * n_cb + cb] = m_new

        pending = {w: scores(units[w]) for w in range(SCORE_LOOKAHEAD)}
        for w in range(len(units)):
            if w + SCORE_LOOKAHEAD < len(units):
                pending[w + SCORE_LOOKAHEAD] = scores(units[w + SCORE_LOOKAHEAD])
            update(units[w], pending.pop(w))

    tile_k = lambda p, n: k_ref[0, p, 0:n, :]
    tile_vd = lambda h, n: vd_ref[0, 0, h, :, 0:n]
    tile_vf = lambda h, n: vf_ref[0, 0, h, :, 0:n]

    @pl.when(kj == 0)
    def _():
        m_scr[...] = jnp.full(m_scr.shape, NEG_INF, F32)
        accd_scr[...] = jnp.zeros(accd_scr.shape, F32)
        accf_scr[...] = jnp.zeros(accf_scr.shape, F32)
        sweep(lambda p, n: km_ref[p], lambda h, n: vdm_ref[h], lambda h, n: vfm_ref[h],
              lambda cb: N_META, None)

    @pl.when(kj < qi)
    def _():
        sweep(tile_k, tile_vd, tile_vf, lambda cb: tk, None)

    @pl.when(kj == qi)
    def _():
        def mask(cb):
            n = (cb + 1) * QCOLS
            key = lax.broadcasted_iota(jnp.int32, (n, QCOLS), 0)
            qry = lax.broadcasted_iota(jnp.int32, (n, QCOLS), 1)
            return key <= qry + cb * QCOLS

        sweep(tile_k, tile_vd, tile_vf, lambda cb: (cb + 1) * QCOLS, mask)

        lam = (jnp.exp(jnp.sum(lq1_ref[...] * lk1_ref[...], axis=1, keepdims=True))
               - jnp.exp(jnp.sum(lq2_ref[...] * lk2_ref[...], axis=1, keepdims=True))
               + lambda_init)
        for h in range(DIFF_HEADS):
            a1 = accd_scr[2 * h]
            a2 = accd_scr[2 * h + 1]
            o = (a1[0:DIFF_V_DIM] / a1[DIFF_V_DIM:DIFF_V_DIM + 1]
                 - lam * (a2[0:DIFF_V_DIM] / a2[DIFF_V_DIM:DIFF_V_DIM + 1]))
            ms = jnp.mean(o * o, axis=0, keepdims=True)
            o = o * lax.rsqrt(ms + SUBLN_EPS) * subg_ref[...] * (1.0 - lambda_init)
            o_ref[0, :, h * 128:(h + 1) * 128] = o.T.astype(o_ref.dtype)
        for i in range(FOX_HEADS // 2):
            a0 = accf_scr[2 * i]
            a1 = accf_scr[2 * i + 1]
            o = jnp.concatenate(
                [a0[0:FOX_HEAD_DIM] / a0[FOX_HEAD_DIM:FOX_HEAD_DIM + 1],
                 a1[0:FOX_HEAD_DIM] / a1[FOX_HEAD_DIM:FOX_HEAD_DIM + 1]], axis=0)
            c0 = DIFF_HEADS * DIFF_V_DIM + i * 128
            o_ref[0, :, c0:c0 + 128] = o.T.astype(o_ref.dtype)


def _attn(qt, k, vd, vf, km, vdm, vfm, lq1, lk1, lq2, lk2, subg, lambda_init, tile):
    b, nq = qt.shape[0], qt.shape[1]
    s = nq * tile
    assert qt.shape[4] == tile and tile % QCOLS == 0
    qi_tab = np.concatenate([np.full(i + 1, i, np.int32) for i in range(nq)])
    kj_tab = np.concatenate([np.arange(i + 1, dtype=np.int32) for i in range(nq)])
    n_steps = int(qi_tab.shape[0])
    d_mix = DIFF_HEADS * DIFF_V_DIM + FOX_HEADS * FOX_HEAD_DIM

    def const(shape):
        nd = len(shape)
        return pl.BlockSpec(shape, lambda *_: (0,) * nd)

    grid_spec = pltpu.PrefetchScalarGridSpec(
        num_scalar_prefetch=2,
        grid=(b, n_steps),
        in_specs=[
            pl.BlockSpec((1, 1, N_QMAPS, 128, tile), lambda i, t, qi, kj: (i, qi[t], 0, 0, 0)),
            pl.BlockSpec((1, N_KBLOCKS, tile, 128), lambda i, t, qi, kj: (i, 0, kj[t], 0)),
            pl.BlockSpec((1, 1, DIFF_HEADS, VD_ROWS, tile),
                         lambda i, t, qi, kj: (i, kj[t], 0, 0, 0)),
            pl.BlockSpec((1, 1, FOX_HEADS, VF_ROWS, tile),
                         lambda i, t, qi, kj: (i, kj[t], 0, 0, 0)),
            const(km.shape), const(vdm.shape), const(vfm.shape),
            const(lq1.shape), const(lk1.shape), const(lq2.shape), const(lk2.shape),
            const(subg.shape),
        ],
        out_specs=pl.BlockSpec((1, tile, d_mix), lambda i, t, qi, kj: (i, qi[t], 0)),
        scratch_shapes=[pltpu.VMEM((N_QMAPS * (tile // QCOLS), 1, QCOLS), F32),
                        pltpu.VMEM((2 * DIFF_HEADS, VD_ROWS, tile), F32),
                        pltpu.VMEM((FOX_HEADS, VF_ROWS, tile), F32)],
    )
    return pl.pallas_call(
        functools.partial(_attn_kernel, lambda_init=lambda_init),
        grid_spec=grid_spec,
        out_shape=jax.ShapeDtypeStruct((b, s, d_mix), BF16),
        compiler_params=pltpu.CompilerParams(dimension_semantics=("parallel", "arbitrary"),
                                             vmem_limit_bytes=VMEM_LIMIT),
        name="attn",
    )(jnp.asarray(qi_tab), jnp.asarray(kj_tab), qt, k, vd, vf, km, vdm, vfm,
      lq1, lk1, lq2, lk2, subg)


def _rope_swap_columns():
    n = 2 * DIFF_HEADS * DIFF_QK_DIM
    idx = np.arange(n)
    sgn = np.zeros(n, np.float32)
    r = idx % DIFF_QK_DIM
    half = ROPE_DIMS // 2
    first = r < half
    second = (r >= half) & (r < ROPE_DIMS)
    src = np.where(first, idx + half, np.where(second, idx - half, idx))
    sgn[first] = -1.0
    sgn[second] = 1.0
    return src.astype(np.int32), sgn


def _placement_matrix():
    m = np.zeros((3 * 128, FOX_HEADS * 128), np.float32)
    for p in range(3):
        for h in range(FOX_HEADS):
            m[p * 128 + h, h * 128 + FOX_HEAD_DIM + p] = 1.0
    return m


def kernel(x, meta_tokens, ffn1_norm_g, ffn1_w_gate_up, ffn1_w_down, mix_norm_g, w_in, b_forget, lam_q1, lam_k1, lam_q2, lam_k2, diff_subln_g, w_out, ffn2_norm_g, ffn2_w_gate_up, ffn2_w_down, final_norm_g):
    b, s, d = x.shape
    assert ffn1_norm_g.shape[0] == 1, "single-layer problem: meta rows are not carried past the mixer"
    assert s % ATTN_TILE == 0 and s % ROW_TILE == 0
    layer = 0
    lambda_init = _lambda_init(layer)

    row = lambda v: v.reshape(1, -1).astype(F32)
    wgu1 = ffn1_w_gate_up[layer].astype(BF16)
    wd1 = ffn1_w_down[layer].astype(BF16)
    wgu2 = ffn2_w_gate_up[layer].astype(BF16)
    wd2 = ffn2_w_down[layer].astype(BF16)
    wout = w_out[layer].astype(BF16)
    w = w_in[layer]
    n512 = 512
    dq_w, dk_w, dv_w, fq_w, fk_w, fv_w = (w[:, i * n512:(i + 1) * n512] for i in range(6))
    fl_w = w[:, 6 * n512:]
    src, sgn = _rope_swap_columns()
    dk_sw = jnp.take(dk_w, jnp.asarray(src), axis=1) * jnp.asarray(sgn)[None, :]
    fk_spread = jnp.pad(fk_w.reshape(d, FOX_HEADS, FOX_HEAD_DIM),
                        ((0, 0), (0, 0), (0, 128 - FOX_HEAD_DIM))).reshape(d, FOX_HEADS * 128)
    fl_blk = jnp.pad(fl_w, ((0, 0), (0, 128 - FOX_HEADS)))
    wrows = jnp.concatenate([dk_w, dk_sw, fk_spread, fl_blk], axis=1).astype(BF16)
    q_scale = QK_SCALE * LOG2E
    wt = jnp.concatenate([dq_w * q_scale, fq_w * q_scale, dv_w, fv_w], axis=1).T.astype(BF16)
    bfg = jnp.pad(b_forget[layer].reshape(1, -1).astype(F32), ((0, 0), (0, 128 - FOX_HEADS)))
    place = jnp.asarray(_placement_matrix()).astype(BF16)
    subg = diff_subln_g[layer].reshape(-1, 1).astype(F32)

    n_pos = max(N_META + s, META_TILE)
    pos = jnp.arange(n_pos, dtype=F32)
    inv_freq = jnp.power(ROPE_THETA, -jnp.arange(0, ROPE_DIMS, 2, dtype=F32) / ROPE_DIMS)
    ang = pos[:, None] * inv_freq[None, :]
    cos, sin = jnp.cos(ang), jnp.sin(ang)
    lane_r = np.arange(128) % DIFF_QK_DIM
    in_rope = jnp.asarray(lane_r < ROPE_DIMS)[None, :]
    lane_f = jnp.asarray(lane_r % (ROPE_DIMS // 2))
    cc_all = jnp.where(in_rope, jnp.take(cos, lane_f, axis=1), 1.0)
    ss_all = jnp.where(in_rope, jnp.take(sin, lane_f, axis=1), 0.0)
    cost_all, sint_all = cos.T, sin.T

    hm = jnp.pad(meta_tokens.astype(F32), ((0, META_TILE - N_META), (0, 0)))
    hm1 = _ffn1(hm, row(ffn1_norm_g[layer]), wgu1, wd1, META_TILE)
    zero_carry = jnp.zeros((8, 128), F32)
    _, km, vdm, vfm, carry_m = _proj(
        hm1[None], row(mix_norm_g[layer]), wrows, wt, bfg,
        cc_all[:META_TILE], ss_all[:META_TILE], cost_all[:, :META_TILE], sint_all[:, :META_TILE],
        zero_carry, place, META_TILE, N_META)

    h1 = _ffn1(x.reshape(b * s, d), row(ffn1_norm_g[layer]), wgu1, wd1, ROW_TILE)
    qt, k, vd, vf, _ = _proj(
        h1.reshape(b, s, d), row(mix_norm_g[layer]), wrows, wt, bfg,
        cc_all[N_META:N_META + s], ss_all[N_META:N_META + s],
        cost_all[:, N_META:N_META + s], sint_all[:, N_META:N_META + s],
        carry_m[0], place, ROW_TILE, ROW_TILE)
    o = _attn(qt, k, vd, vf, km[0, :, :N_META, :], vdm[0, 0, :, :, :N_META], vfm[0, 0, :, :, :N_META],
              row(lam_q1[layer]), row(lam_k1[layer]), row(lam_q2[layer]), row(lam_k2[layer]),
              subg, lambda_init, ATTN_TILE)
    out = _tail(o.reshape(b * s, -1), h1, wout, row(ffn2_norm_g[layer]), wgu2, wd2,
                row(final_norm_g), ROW_TILE)
    return out.reshape(b, s, d)
```

```python
import functools
import math

import numpy as np
import jax
import jax.numpy as jnp
from jax import lax
from jax.experimental import pallas as pl
from jax.experimental.pallas import tpu as pltpu

F32 = jnp.float32
BF16 = jnp.bfloat16

N_META = 16
DIFF_HEADS = 4
DIFF_V_DIM = 128
DIFF_QK_DIM = 64
FOX_HEADS = 8
FOX_HEAD_DIM = 64
ROPE_DIMS = 16
ROPE_THETA = 500000.0
RMS_EPS = 1e-6
SUBLN_EPS = 1e-5
NEG_INF = -1e30
QK_SCALE = 0.125
LOG2E = math.log2(math.e)

N_QMAPS = 2 * DIFF_HEADS + FOX_HEADS
N_KBLOCKS = DIFF_HEADS + FOX_HEADS
BF16_SUBLANES = 16
VD_ROWS = DIFF_V_DIM + BF16_SUBLANES
VF_ROWS = FOX_HEAD_DIM + BF16_SUBLANES
META_TILE = 128
FFN_CHUNK = 256
ROW_TILE = 512
ATTN_TILE = 512
QCOLS = 256
KV_GROUP = 4
SCORE_LOOKAHEAD = 3
META_LOOKAHEAD = 12
VMEM_LIMIT = 56 * 1024 * 1024


def _lambda_init(layer_idx):
    return 0.8 - 0.6 * math.exp(-0.3 * layer_idx)


def _rms(x, g, eps):
    ms = jnp.mean(x * x, axis=-1, keepdims=True)
    return x * lax.rsqrt(ms + eps) * g


def _split3(x):
    hi = x.astype(BF16)
    r = x - hi.astype(F32)
    mid = r.astype(BF16)
    lo = (r - mid.astype(F32)).astype(BF16)
    return hi, mid, lo


def _const_spec(shape):
    nd = len(shape)
    return pl.BlockSpec(shape, lambda *_: (0,) * nd, pipeline_mode=pl.Buffered(1))


def _swiglu_half_step(h, g_ref, wgu_ref, wd_ref):
    d_ff = wd_ref.shape[0]
    hn = _rms(h, g_ref[...], RMS_EPS).astype(BF16)
    acc = jnp.zeros(h.shape, F32)
    for c in range(d_ff // FFN_CHUNK):
        lo, hi = c * FFN_CHUNK, (c + 1) * FFN_CHUNK
        gate = jnp.dot(hn, wgu_ref[:, lo:hi], preferred_element_type=F32)
        up = jnp.dot(hn, wgu_ref[:, d_ff + lo:d_ff + hi], preferred_element_type=F32)
        act = (gate / (1.0 + jnp.exp(-gate)) * up).astype(BF16)
        acc = acc + jnp.dot(act, wd_ref[lo:hi, :], preferred_element_type=F32)
    return h + 0.5 * acc


def _ffn1_kernel(h_ref, g_ref, wgu_ref, wd_ref, o_ref):
    o_ref[...] = _swiglu_half_step(h_ref[...], g_ref, wgu_ref, wd_ref)


def _ffn1(h, g, wgu, wd, tm):
    rows, d = h.shape
    return pl.pallas_call(
        _ffn1_kernel,
        grid=(rows // tm,),
        in_specs=[pl.BlockSpec((tm, d), lambda i: (i, 0)),
                  _const_spec(g.shape), _const_spec(wgu.shape), _const_spec(wd.shape)],
        out_specs=pl.BlockSpec((tm, d), lambda i: (i, 0)),
        out_shape=jax.ShapeDtypeStruct((rows, d), F32),
        compiler_params=pltpu.CompilerParams(dimension_semantics=("parallel",),
                                             vmem_limit_bytes=VMEM_LIMIT),
        name="ffn1",
    )(h, g, wgu, wd)


def _tail_kernel(o_ref, h_ref, wout_ref, g_ref, wgu_ref, wd_ref, gf_ref, out_ref):
    h2 = h_ref[...] + jnp.dot(o_ref[...], wout_ref[...], preferred_element_type=F32)
    h3 = _swiglu_half_step(h2, g_ref, wgu_ref, wd_ref)
    out_ref[...] = _rms(h3, gf_ref[...], RMS_EPS)


def _tail(o, h1, wout, g, wgu, wd, gf, tm):
    rows, d = h1.shape
    return pl.pallas_call(
        _tail_kernel,
        grid=(rows // tm,),
        in_specs=[pl.BlockSpec((tm, d), lambda i: (i, 0)),
                  pl.BlockSpec((tm, d), lambda i: (i, 0)),
                  _const_spec(wout.shape), _const_spec(g.shape), _const_spec(wgu.shape),
                  _const_spec(wd.shape), _const_spec(gf.shape)],
        out_specs=pl.BlockSpec((tm, d), lambda i: (i, 0)),
        out_shape=jax.ShapeDtypeStruct((rows, d), F32),
        compiler_params=pltpu.CompilerParams(dimension_semantics=("parallel",),
                                             vmem_limit_bytes=VMEM_LIMIT),
        name="tail",
    )(o, h1, wout, g, wgu, wd, gf)


def _proj_kernel(h_ref, g_ref, wrows_ref, wt_ref, bf_ref, cc_ref, ss_ref, cost_ref, sint_ref,
                 carry0_ref, place_ref,
                 qt_ref, k_ref, vd_ref, vf_ref, carry_out_ref, carry_scr, *, n_valid):
    tm = h_ref.shape[1]

    @pl.when(pl.program_id(1) == 0)
    def _():
        carry_scr[...] = carry0_ref[...]

    hn = _rms(h_ref[0], g_ref[...], RMS_EPS).astype(BF16)
    kr = jnp.dot(hn, wrows_ref[...], preferred_element_type=F32)
    pt = lax.dot_general(wt_ref[...], hn, (((1,), (1,)), ((), ())),
                         preferred_element_type=F32)

    cc = cc_ref[...]
    ss = ss_ref[...]
    n_dk = DIFF_HEADS * 128
    for p in range(DIFF_HEADS):
        kd = kr[:, p * 128:(p + 1) * 128]
        ks = kr[:, n_dk + p * 128:n_dk + (p + 1) * 128]
        k_ref[0, p] = (kd * cc + ks * ss).astype(BF16)

    fk0 = 2 * n_dk
    fl0 = fk0 + FOX_HEADS * 128
    fl = kr[:, fl0:fl0 + 128] + bf_ref[...]
    logf = jnp.minimum(fl, 0.0) - jnp.log(1.0 + jnp.exp(-jnp.abs(fl)))
    lane = lax.broadcasted_iota(jnp.int32, (tm, 128), 1)
    keep = lane < FOX_HEADS
    if n_valid < tm:
        row = lax.broadcasted_iota(jnp.int32, (tm, 128), 0)
        keep = jnp.logical_and(keep, row < n_valid)
    logf = jnp.where(keep, logf, 0.0)
    parts = jnp.concatenate(_split3(logf), axis=1)
    r_i = lax.broadcasted_iota(jnp.int32, (tm, tm), 0)
    c_i = lax.broadcasted_iota(jnp.int32, (tm, tm), 1)
    tri = jnp.where(r_i >= c_i, 1.0, 0.0).astype(BF16)
    cs = jnp.dot(tri, parts, preferred_element_type=F32)
    cum = cs[:, 0:128] + cs[:, 128:256] + cs[:, 256:384] + carry_scr[0:1, :]
    carry_scr[...] = jnp.broadcast_to(cum[tm - 1:tm, :], carry_scr.shape)
    carry_out_ref[0] = carry_scr[...]
    cparts = jnp.concatenate(_split3(cum * LOG2E), axis=1)
    aug = jnp.dot(cparts, place_ref[...], preferred_element_type=F32)
    for hh in range(FOX_HEADS):
        kf = kr[:, fk0 + hh * 128:fk0 + (hh + 1) * 128] + aug[:, hh * 128:(hh + 1) * 128]
        k_ref[0, DIFF_HEADS + hh] = kf.astype(BF16)

    cost = cost_ref[...]
    sint = sint_ref[...]
    half = ROPE_DIMS // 2
    zeros64 = jnp.zeros((64, tm), BF16)
    for h in range(DIFF_HEADS):
        for c in range(2):
            base = h * 128 + c * 64
            x1 = pt[base:base + half]
            x2 = pt[base + half:base + ROPE_DIMS]
            blk = jnp.concatenate([x1 * cost - x2 * sint, x2 * cost + x1 * sint,
                                   pt[base + ROPE_DIMS:base + 64]], axis=0).astype(BF16)
            u = 2 * h + c
            qt_ref[0, 0, u, c * 64:(c + 1) * 64, :] = blk
            qt_ref[0, 0, u, (1 - c) * 64:(2 - c) * 64, :] = zeros64
    fq0 = DIFF_HEADS * 128
    rows64 = lax.broadcasted_iota(jnp.int32, (64, tm), 0)
    neg_rows = jnp.where(rows64 < 3, -1.0, 0.0).astype(BF16)
    for hh in range(FOX_HEADS):
        u = 2 * DIFF_HEADS + hh
        qt_ref[0, 0, u, 0:64, :] = pt[fq0 + hh * 64:fq0 + (hh + 1) * 64].astype(BF16)
        qt_ref[0, 0, u, 64:128, :] = neg_rows

    ones_rows = jnp.ones((BF16_SUBLANES, tm), BF16)
    dv0 = fq0 + FOX_HEADS * 64
    for h in range(DIFF_HEADS):
        vd_ref[0, 0, h, 0:DIFF_V_DIM, :] = pt[dv0 + h * 128:dv0 + (h + 1) * 128].astype(BF16)
        vd_ref[0, 0, h, DIFF_V_DIM:VD_ROWS, :] = ones_rows
    fv0 = dv0 + DIFF_HEADS * 128
    for hh in range(FOX_HEADS):
        vf_ref[0, 0, hh, 0:FOX_HEAD_DIM, :] = pt[fv0 + hh * 64:fv0 + (hh + 1) * 64].astype(BF16)
        vf_ref[0, 0, hh, FOX_HEAD_DIM:VF_ROWS, :] = ones_rows


def _proj(h, g, wrows, wt, bfg, cc, ss, cost, sint, carry0, place, tm, n_valid):
    b, s, d = h.shape
    nt = s // tm
    out_shape = (
        jax.ShapeDtypeStruct((b, nt, N_QMAPS, 128, tm), BF16),
        jax.ShapeDtypeStruct((b, N_KBLOCKS, s, 128), BF16),
        jax.ShapeDtypeStruct((b, nt, DIFF_HEADS, VD_ROWS, tm), BF16),
        jax.ShapeDtypeStruct((b, nt, FOX_HEADS, VF_ROWS, tm), BF16),
        jax.ShapeDtypeStruct((b, 8, 128), F32),
    )
    return pl.pallas_call(
        functools.partial(_proj_kernel, n_valid=n_valid),
        grid=(b, nt),
        in_specs=[pl.BlockSpec((1, tm, d), lambda i, t: (i, t, 0)),
                  _const_spec(g.shape), _const_spec(wrows.shape), _const_spec(wt.shape),
                  _const_spec(bfg.shape),
                  pl.BlockSpec((tm, 128), lambda i, t: (t, 0)),
                  pl.BlockSpec((tm, 128), lambda i, t: (t, 0)),
                  pl.BlockSpec((8, tm), lambda i, t: (0, t)),
                  pl.BlockSpec((8, tm), lambda i, t: (0, t)),
                  _const_spec(carry0.shape), _const_spec(place.shape)],
        out_specs=(pl.BlockSpec((1, 1, N_QMAPS, 128, tm), lambda i, t: (i, t, 0, 0, 0)),
                   pl.BlockSpec((1, N_KBLOCKS, tm, 128), lambda i, t: (i, 0, t, 0)),
                   pl.BlockSpec((1, 1, DIFF_HEADS, VD_ROWS, tm), lambda i, t: (i, t, 0, 0, 0)),
                   pl.BlockSpec((1, 1, FOX_HEADS, VF_ROWS, tm), lambda i, t: (i, t, 0, 0, 0)),
                   pl.BlockSpec((1, 8, 128), lambda i, t: (i, 0, 0))),
        out_shape=out_shape,
        scratch_shapes=[pltpu.VMEM((8, 128), F32)],
        compiler_params=pltpu.CompilerParams(dimension_semantics=("parallel", "arbitrary"),
                                             vmem_limit_bytes=VMEM_LIMIT),
        name="proj",
    )(h, g, wrows, wt, bfg, cc, ss, cost, sint, carry0, place)


def _attn_kernel(qi_ref, kg_ref,
                 qt_ref, k_ref, vd_ref, vf_ref, km_ref, vdm_ref, vfm_ref,
                 lq1_ref, lk1_ref, lq2_ref, lk2_ref, subg_ref,
                 o_ref, m_scr, accd_scr, accf_scr, *, lambda_init):
    s_idx = pl.program_id(1)
    qi = qi_ref[s_idx]
    kg = kg_ref[s_idx]
    tq = qt_ref.shape[4]
    n_group = vd_ref.shape[1]
    tk = k_ref.shape[2] // n_group
    assert tq % QCOLS == 0 and tq == tk

    n_dmaps = 2 * DIFF_HEADS
    n_cb = tq // QCOLS
    units = [(u, cb) for u in range(N_QMAPS) for cb in range(n_cb)]

    def sweep(k_rows, vd_cols, vf_cols, n_keys, mask, first=False, lookahead=SCORE_LOOKAHEAD):
        def scores(unit):
            u, cb = unit
            kb = u // 2 if u < n_dmaps else u - DIFF_HEADS
            return jnp.dot(k_rows(kb, n_keys(cb)),
                           qt_ref[0, 0, u, :, cb * QCOLS:(cb + 1) * QCOLS],
                           preferred_element_type=F32)

        def update(unit, st):
            u, cb = unit
            cols = slice(cb * QCOLS, (cb + 1) * QCOLS)
            if u < n_dmaps:
                acc_scr, a, v_t = accd_scr, u, vd_cols(u // 2, n_keys(cb))
            else:
                acc_scr, a, v_t = accf_scr, u - n_dmaps, vf_cols(u - n_dmaps, n_keys(cb))
            if mask is not None:
                st = jnp.where(mask(cb), st, NEG_INF)
            m_new = jnp.max(st, axis=0, keepdims=True)
            if not first:
                m_old = m_scr[u * n_cb + cb]
                m_new = jnp.maximum(m_old, m_new)
            p = jnp.exp2(st - m_new).astype(BF16)
            pv = jnp.dot(v_t, p, preferred_element_type=F32)
            if first:
                acc_scr[a, :, cols] = pv
            else:
                acc_scr[a, :, cols] = acc_scr[a, :, cols] * jnp.exp2(m_old - m_new) + pv
            m_scr[u * n_cb + cb] = m_new

        pending = {w: scores(units[w]) for w in range(lookahead)}
        for w in range(len(units)):
            if w + lookahead < len(units):
                pending[w + lookahead] = scores(units[w + lookahead])
            update(units[w], pending.pop(w))

    def key_tile(g):
        row0 = pl.multiple_of(g * tk, tk)
        return (lambda p, n: k_ref[0, p, pl.ds(row0, n), :],
                lambda h, n: vd_ref[0, g, h, :, 0:n],
                lambda h, n: vf_ref[0, g, h, :, 0:n])

    @pl.when(kg == 0)
    def _():
        sweep(lambda p, n: km_ref[p], lambda h, n: vdm_ref[h], lambda h, n: vfm_ref[h],
              lambda cb: N_META, None, first=True, lookahead=META_LOOKAHEAD)

    first = kg * n_group

    def full_tile(g, carry):
        sweep(*key_tile(g), lambda cb: tk, None)
        return carry

    lax.fori_loop(0, jnp.clip(qi - first, 0, n_group), full_tile, 0)

    @pl.when(jnp.logical_and(qi >= first, qi < first + n_group))
    def _():
        def mask(cb):
            n = (cb + 1) * QCOLS
            key = lax.broadcasted_iota(jnp.int32, (n, QCOLS), 0)
            qry = lax.broadcasted_iota(jnp.int32, (n, QCOLS), 1)
            return key <= qry + cb * QCOLS

        sweep(*key_tile(qi - first), lambda cb: (cb + 1) * QCOLS, mask)

        lam = (jnp.exp(jnp.sum(lq1_ref[...] * lk1_ref[...], axis=1, keepdims=True))
               - jnp.exp(jnp.sum(lq2_ref[...] * lk2_ref[...], axis=1, keepdims=True))
               + lambda_init)
        for h in range(DIFF_HEADS):
            a1 = accd_scr[2 * h]
            a2 = accd_scr[2 * h + 1]
            o = (a1[0:DIFF_V_DIM] / a1[DIFF_V_DIM:DIFF_V_DIM + 1]
                 - lam * (a2[0:DIFF_V_DIM] / a2[DIFF_V_DIM:DIFF_V_DIM + 1]))
            ms = jnp.mean(o * o, axis=0, keepdims=True)
            o = o * lax.rsqrt(ms + SUBLN_EPS) * subg_ref[...] * (1.0 - lambda_init)
            o_ref[0, :, h * 128:(h + 1) * 128] = o.T.astype(o_ref.dtype)
        for i in range(FOX_HEADS // 2):
            a0 = accf_scr[2 * i]
            a1 = accf_scr[2 * i + 1]
            o = jnp.concatenate(
                [a0[0:FOX_HEAD_DIM] / a0[FOX_HEAD_DIM:FOX_HEAD_DIM + 1],
                 a1[0:FOX_HEAD_DIM] / a1[FOX_HEAD_DIM:FOX_HEAD_DIM + 1]], axis=0)
            c0 = DIFF_HEADS * DIFF_V_DIM + i * 128
            o_ref[0, :, c0:c0 + 128] = o.T.astype(o_ref.dtype)


def _attn(qt, k, vd, vf, km, vdm, vfm, lq1, lk1, lq2, lk2, subg, lambda_init, tile):
    b, nq = qt.shape[0], qt.shape[1]
    s = nq * tile
    grp = min(KV_GROUP, nq)
    assert qt.shape[4] == tile and tile % QCOLS == 0 and nq % grp == 0
    qi_tab = np.concatenate([np.full(i // grp + 1, i, np.int32) for i in range(nq)])
    kg_tab = np.concatenate([np.arange(i // grp + 1, dtype=np.int32) for i in range(nq)])
    n_steps = int(qi_tab.shape[0])
    d_mix = DIFF_HEADS * DIFF_V_DIM + FOX_HEADS * FOX_HEAD_DIM

    def const(shape):
        nd = len(shape)
        return pl.BlockSpec(shape, lambda *_: (0,) * nd)

    grid_spec = pltpu.PrefetchScalarGridSpec(
        num_scalar_prefetch=2,
        grid=(b, n_steps),
        in_specs=[
            pl.BlockSpec((1, 1, N_QMAPS, 128, tile), lambda i, t, qi, kg: (i, qi[t], 0, 0, 0)),
            pl.BlockSpec((1, N_KBLOCKS, grp * tile, 128), lambda i, t, qi, kg: (i, 0, kg[t], 0)),
            pl.BlockSpec((1, grp, DIFF_HEADS, VD_ROWS, tile),
                         lambda i, t, qi, kg: (i, kg[t], 0, 0, 0)),
            pl.BlockSpec((1, grp, FOX_HEADS, VF_ROWS, tile),
                         lambda i, t, qi, kg: (i, kg[t], 0, 0, 0)),
            const(km.shape), const(vdm.shape), const(vfm.shape),
            const(lq1.shape), const(lk1.shape), const(lq2.shape), const(lk2.shape),
            const(subg.shape),
        ],
        out_specs=pl.BlockSpec((1, tile, d_mix), lambda i, t, qi, kg: (i, qi[t], 0)),
        scratch_shapes=[pltpu.VMEM((N_QMAPS * (tile // QCOLS), 1, QCOLS), F32),
                        pltpu.VMEM((2 * DIFF_HEADS, VD_ROWS, tile), F32),
                        pltpu.VMEM((FOX_HEADS, VF_ROWS, tile), F32)],
    )
    return pl.pallas_call(
        functools.partial(_attn_kernel, lambda_init=lambda_init),
        grid_spec=grid_spec,
        out_shape=jax.ShapeDtypeStruct((b, s, d_mix), BF16),
        compiler_params=pltpu.CompilerParams(dimension_semantics=("parallel", "arbitrary"),
                                             vmem_limit_bytes=VMEM_LIMIT),
        name="attn",
    )(jnp.asarray(qi_tab), jnp.asarray(kg_tab), qt, k, vd, vf, km, vdm, vfm,
      lq1, lk1, lq2, lk2, subg)


def _rope_swap_columns():
    n = 2 * DIFF_HEADS * DIFF_QK_DIM
    idx = np.arange(n)
    sgn = np.zeros(n, np.float32)
    r = idx % DIFF_QK_DIM
    half = ROPE_DIMS // 2
    first = r < half
    second = (r >= half) & (r < ROPE_DIMS)
    src = np.where(first, idx + half, np.where(second, idx - half, idx))
    sgn[first] = -1.0
    sgn[second] = 1.0
    return src.astype(np.int32), sgn


def _placement_matrix():
    m = np.zeros((3 * 128, FOX_HEADS * 128), np.float32)
    for p in range(3):
        for h in range(FOX_HEADS):
            m[p * 128 + h, h * 128 + FOX_HEAD_DIM + p] = 1.0
    return m


def kernel(x, meta_tokens, ffn1_norm_g, ffn1_w_gate_up, ffn1_w_down, mix_norm_g, w_in, b_forget, lam_q1, lam_k1, lam_q2, lam_k2, diff_subln_g, w_out, ffn2_norm_g, ffn2_w_gate_up, ffn2_w_down, final_norm_g):
    b, s, d = x.shape
    assert ffn1_norm_g.shape[0] == 1, "single-layer problem: meta rows are not carried past the mixer"
    assert s % ATTN_TILE == 0 and s % ROW_TILE == 0
    layer = 0
    lambda_init = _lambda_init(layer)

    row = lambda v: v.reshape(1, -1).astype(F32)
    wgu1 = ffn1_w_gate_up[layer].astype(BF16)
    wd1 = ffn1_w_down[layer].astype(BF16)
    wgu2 = ffn2_w_gate_up[layer].astype(BF16)
    wd2 = ffn2_w_down[layer].astype(BF16)
    wout = w_out[layer].astype(BF16)
    w = w_in[layer]
    n512 = 512
    dq_w, dk_w, dv_w, fq_w, fk_w, fv_w = (w[:, i * n512:(i + 1) * n512] for i in range(6))
    fl_w = w[:, 6 * n512:]
    src, sgn = _rope_swap_columns()
    dk_sw = jnp.take(dk_w, jnp.asarray(src), axis=1) * jnp.asarray(sgn)[None, :]
    fk_spread = jnp.pad(fk_w.reshape(d, FOX_HEADS, FOX_HEAD_DIM),
                        ((0, 0), (0, 0), (0, 128 - FOX_HEAD_DIM))).reshape(d, FOX_HEADS * 128)
    fl_blk = jnp.pad(fl_w, ((0, 0), (0, 128 - FOX_HEADS)))
    wrows = jnp.concatenate([dk_w, dk_sw, fk_spread, fl_blk], axis=1).astype(BF16)
    q_scale = QK_SCALE * LOG2E
    wt = jnp.concatenate([dq_w * q_scale, fq_w * q_scale, dv_w, fv_w], axis=1).T.astype(BF16)
    bfg = jnp.pad(b_forget[layer].reshape(1, -1).astype(F32), ((0, 0), (0, 128 - FOX_HEADS)))
    place = jnp.asarray(_placement_matrix()).astype(BF16)
    subg = diff_subln_g[layer].reshape(-1, 1).astype(F32)

    n_pos = max(N_META + s, META_TILE)
    pos = jnp.arange(n_pos, dtype=F32)
    inv_freq = jnp.power(ROPE_THETA, -jnp.arange(0, ROPE_DIMS, 2, dtype=F32) / ROPE_DIMS)
    ang = pos[:, None] * inv_freq[None, :]
    cos, sin = jnp.cos(ang), jnp.sin(ang)
    lane_r = np.arange(128) % DIFF_QK_DIM
    in_rope = jnp.asarray(lane_r < ROPE_DIMS)[None, :]
    lane_f = jnp.asarray(lane_r % (ROPE_DIMS // 2))
    cc_all = jnp.where(in_rope, jnp.take(cos, lane_f, axis=1), 1.0)
    ss_all = jnp.where(in_rope, jnp.take(sin, lane_f, axis=1), 0.0)
    cost_all, sint_all = cos.T, sin.T

    hm = jnp.pad(meta_tokens.astype(F32), ((0, META_TILE - N_META), (0, 0)))
    hm1 = _ffn1(hm, row(ffn1_norm_g[layer]), wgu1, wd1, META_TILE)
    zero_carry = jnp.zeros((8, 128), F32)
    _, km, vdm, vfm, carry_m = _proj(
        hm1[None], row(mix_norm_g[layer]), wrows, wt, bfg,
        cc_all[:META_TILE], ss_all[:META_TILE], cost_all[:, :META_TILE], sint_all[:, :META_TILE],
        zero_carry, place, META_TILE, N_META)

    h1 = _ffn1(x.reshape(b * s, d), row(ffn1_norm_g[layer]), wgu1, wd1, ROW_TILE)
    qt, k, vd, vf, _ = _proj(
        h1.reshape(b, s, d), row(mix_norm_g[layer]), wrows, wt, bfg,
        cc_all[N_META:N_META + s], ss_all[N_META:N_META + s],
        cost_all[:, N_META:N_META + s], sint_all[:, N_META:N_META + s],
        carry_m[0], place, ROW_TILE, ROW_TILE)
    o = _attn(qt, k, vd, vf, km[0, :, :N_META, :], vdm[0, 0, :, :, :N_META], vfm[0, 0, :, :, :N_META],
              row(lam_q1[layer]), row(lam_k1[layer]), row(lam_q2[layer]), row(lam_k2[layer]),
              subg, lambda_init, ATTN_TILE)
    out = _tail(o.reshape(b * s, -1), h1, wout, row(ffn2_norm_g[layer]), wgu2, wd2,
                row(final_norm_g), ROW_TILE)
    return out.reshape(b, s, d)
```

```python
import functools
import math

import numpy as np
import jax
import jax.numpy as jnp
from jax import lax
from jax.experimental import pallas as pl
from jax.experimental.pallas import tpu as pltpu

F32 = jnp.float32
BF16 = jnp.bfloat16

N_META = 16
DIFF_HEADS = 4
DIFF_V_DIM = 128
DIFF_QK_DIM = 64
FOX_HEADS = 8
FOX_HEAD_DIM = 64
ROPE_DIMS = 16
ROPE_THETA = 500000.0
RMS_EPS = 1e-6
SUBLN_EPS = 1e-5
NEG_INF = -1e30
QK_SCALE = 0.125
LOG2E = math.log2(math.e)

N_QMAPS = 2 * DIFF_HEADS + FOX_HEADS
N_KBLOCKS = DIFF_HEADS + FOX_HEADS
BF16_SUBLANES = 16
VD_ROWS = DIFF_V_DIM + BF16_SUBLANES
VF_ROWS = FOX_HEAD_DIM + BF16_SUBLANES
META_TILE = 128
FFN_CHUNK = 256
ROW_TILE = 512
ATTN_TILE = 512
QCOLS = 256
FAST_OVER = 20.0
FAST_UNDER = 40.0
KV_GROUP = 4
SCORE_LOOKAHEAD = 2
META_LOOKAHEAD = 12
VMEM_LIMIT = 56 * 1024 * 1024


def _lambda_init(layer_idx):
    return 0.8 - 0.6 * math.exp(-0.3 * layer_idx)


def _rms(x, g, eps):
    ms = jnp.mean(x * x, axis=-1, keepdims=True)
    return x * lax.rsqrt(ms + eps) * g


def _split3(x):
    hi = x.astype(BF16)
    r = x - hi.astype(F32)
    mid = r.astype(BF16)
    lo = (r - mid.astype(F32)).astype(BF16)
    return hi, mid, lo


def _const_spec(shape):
    nd = len(shape)
    return pl.BlockSpec(shape, lambda *_: (0,) * nd, pipeline_mode=pl.Buffered(1))


def _swiglu_half_step(h, g_ref, wgu_ref, wd_ref):
    d_ff = wd_ref.shape[0]
    hn = _rms(h, g_ref[...], RMS_EPS).astype(BF16)
    acc = jnp.zeros(h.shape, F32)
    for c in range(d_ff // FFN_CHUNK):
        lo, hi = c * FFN_CHUNK, (c + 1) * FFN_CHUNK
        gate = jnp.dot(hn, wgu_ref[:, lo:hi], preferred_element_type=F32)
        up = jnp.dot(hn, wgu_ref[:, d_ff + lo:d_ff + hi], preferred_element_type=F32)
        act = (gate / (1.0 + jnp.exp(-gate)) * up).astype(BF16)
        acc = acc + jnp.dot(act, wd_ref[lo:hi, :], preferred_element_type=F32)
    return h + 0.5 * acc


def _ffn1_kernel(h_ref, g_ref, wgu_ref, wd_ref, o_ref):
    o_ref[...] = _swiglu_half_step(h_ref[...], g_ref, wgu_ref, wd_ref)


def _ffn1(h, g, wgu, wd, tm):
    rows, d = h.shape
    return pl.pallas_call(
        _ffn1_kernel,
        grid=(rows // tm,),
        in_specs=[pl.BlockSpec((tm, d), lambda i: (i, 0)),
                  _const_spec(g.shape), _const_spec(wgu.shape), _const_spec(wd.shape)],
        out_specs=pl.BlockSpec((tm, d), lambda i: (i, 0)),
        out_shape=jax.ShapeDtypeStruct((rows, d), F32),
        compiler_params=pltpu.CompilerParams(dimension_semantics=("parallel",),
                                             vmem_limit_bytes=VMEM_LIMIT),
        name="ffn1",
    )(h, g, wgu, wd)


def _tail_kernel(o_ref, h_ref, wout_ref, g_ref, wgu_ref, wd_ref, gf_ref, out_ref):
    h2 = h_ref[...] + jnp.dot(o_ref[...], wout_ref[...], preferred_element_type=F32)
    h3 = _swiglu_half_step(h2, g_ref, wgu_ref, wd_ref)
    out_ref[...] = _rms(h3, gf_ref[...], RMS_EPS)


def _tail(o, h1, wout, g, wgu, wd, gf, tm):
    rows, d = h1.shape
    return pl.pallas_call(
        _tail_kernel,
        grid=(rows // tm,),
        in_specs=[pl.BlockSpec((tm, d), lambda i: (i, 0)),
                  pl.BlockSpec((tm, d), lambda i: (i, 0)),
                  _const_spec(wout.shape), _const_spec(g.shape), _const_spec(wgu.shape),
                  _const_spec(wd.shape), _const_spec(gf.shape)],
        out_specs=pl.BlockSpec((tm, d), lambda i: (i, 0)),
        out_shape=jax.ShapeDtypeStruct((rows, d), F32),
        compiler_params=pltpu.CompilerParams(dimension_semantics=("parallel",),
                                             vmem_limit_bytes=VMEM_LIMIT),
        name="tail",
    )(o, h1, wout, g, wgu, wd, gf)


def _proj_kernel(h_ref, g_ref, wrows_ref, wt_ref, bf_ref, cc_ref, ss_ref, cost_ref, sint_ref,
                 carry0_ref, place_ref,
                 qt_ref, k_ref, vd_ref, vf_ref, cend_ref, carry_scr, *, n_valid):
    tm = h_ref.shape[1]

    @pl.when(pl.program_id(1) == 0)
    def _():
        carry_scr[...] = carry0_ref[...]

    hn = _rms(h_ref[0], g_ref[...], RMS_EPS).astype(BF16)
    kr = jnp.dot(hn, wrows_ref[...], preferred_element_type=F32)
    pt = lax.dot_general(wt_ref[...], hn, (((1,), (1,)), ((), ())),
                         preferred_element_type=F32)

    cc = cc_ref[...]
    ss = ss_ref[...]
    n_dk = DIFF_HEADS * 128
    for p in range(DIFF_HEADS):
        kd = kr[:, p * 128:(p + 1) * 128]
        ks = kr[:, n_dk + p * 128:n_dk + (p + 1) * 128]
        k_ref[0, p] = (kd * cc + ks * ss).astype(BF16)

    fk0 = 2 * n_dk
    fl0 = fk0 + FOX_HEADS * 128
    fl = kr[:, fl0:fl0 + 128] + bf_ref[...]
    logf = jnp.minimum(fl, 0.0) - jnp.log(1.0 + jnp.exp(-jnp.abs(fl)))
    lane = lax.broadcasted_iota(jnp.int32, (tm, 128), 1)
    keep = lane < FOX_HEADS
    if n_valid < tm:
        row = lax.broadcasted_iota(jnp.int32, (tm, 128), 0)
        keep = jnp.logical_and(keep, row < n_valid)
    logf = jnp.where(keep, logf, 0.0)
    parts = jnp.concatenate(_split3(logf), axis=1)
    r_i = lax.broadcasted_iota(jnp.int32, (tm, tm), 0)
    c_i = lax.broadcasted_iota(jnp.int32, (tm, tm), 1)
    tri = jnp.where(r_i >= c_i, 1.0, 0.0).astype(BF16)
    cs = jnp.dot(tri, parts, preferred_element_type=F32)
    cum = cs[:, 0:128] + cs[:, 128:256] + cs[:, 256:384] + carry_scr[0:1, :]
    carry_scr[...] = jnp.broadcast_to(cum[tm - 1:tm, :], carry_scr.shape)
    cend_ref[0, 0] = carry_scr[...]
    cparts = jnp.concatenate(_split3(cum * LOG2E), axis=1)
    aug = jnp.dot(cparts, place_ref[...], preferred_element_type=F32)
    for hh in range(FOX_HEADS):
        kf = kr[:, fk0 + hh * 128:fk0 + (hh + 1) * 128] + aug[:, hh * 128:(hh + 1) * 128]
        k_ref[0, DIFF_HEADS + hh] = kf.astype(BF16)

    cost = cost_ref[...]
    sint = sint_ref[...]
    half = ROPE_DIMS // 2
    zeros64 = jnp.zeros((64, tm), BF16)
    for h in range(DIFF_HEADS):
        for c in range(2):
            base = h * 128 + c * 64
            x1 = pt[base:base + half]
            x2 = pt[base + half:base + ROPE_DIMS]
            blk = jnp.concatenate([x1 * cost - x2 * sint, x2 * cost + x1 * sint,
                                   pt[base + ROPE_DIMS:base + 64]], axis=0).astype(BF16)
            u = 2 * h + c
            qt_ref[0, 0, u, c * 64:(c + 1) * 64, :] = blk
            qt_ref[0, 0, u, (1 - c) * 64:(2 - c) * 64, :] = zeros64
    fq0 = DIFF_HEADS * 128
    rows64 = lax.broadcasted_iota(jnp.int32, (64, tm), 0)
    neg_rows = jnp.where(rows64 < 3, -1.0, 0.0).astype(BF16)
    for hh in range(FOX_HEADS):
        u = 2 * DIFF_HEADS + hh
        qt_ref[0, 0, u, 0:64, :] = pt[fq0 + hh * 64:fq0 + (hh + 1) * 64].astype(BF16)
        qt_ref[0, 0, u, 64:128, :] = neg_rows

    ones_rows = jnp.ones((BF16_SUBLANES, tm), BF16)
    dv0 = fq0 + FOX_HEADS * 64
    for h in range(DIFF_HEADS):
        vd_ref[0, 0, h, 0:DIFF_V_DIM, :] = pt[dv0 + h * 128:dv0 + (h + 1) * 128].astype(BF16)
        vd_ref[0, 0, h, DIFF_V_DIM:VD_ROWS, :] = ones_rows
    fv0 = dv0 + DIFF_HEADS * 128
    for hh in range(FOX_HEADS):
        vf_ref[0, 0, hh, 0:FOX_HEAD_DIM, :] = pt[fv0 + hh * 64:fv0 + (hh + 1) * 64].astype(BF16)
        vf_ref[0, 0, hh, FOX_HEAD_DIM:VF_ROWS, :] = ones_rows


def _proj(h, g, wrows, wt, bfg, cc, ss, cost, sint, carry0, place, tm, n_valid):
    b, s, d = h.shape
    nt = s // tm
    out_shape = (
        jax.ShapeDtypeStruct((b, nt, N_QMAPS, 128, tm), BF16),
        jax.ShapeDtypeStruct((b, N_KBLOCKS, s, 128), BF16),
        jax.ShapeDtypeStruct((b, nt, DIFF_HEADS, VD_ROWS, tm), BF16),
        jax.ShapeDtypeStruct((b, nt, FOX_HEADS, VF_ROWS, tm), BF16),
        jax.ShapeDtypeStruct((b, nt, 8, 128), F32),
    )
    return pl.pallas_call(
        functools.partial(_proj_kernel, n_valid=n_valid),
        grid=(b, nt),
        in_specs=[pl.BlockSpec((1, tm, d), lambda i, t: (i, t, 0)),
                  _const_spec(g.shape), _const_spec(wrows.shape), _const_spec(wt.shape),
                  _const_spec(bfg.shape),
                  pl.BlockSpec((tm, 128), lambda i, t: (t, 0)),
                  pl.BlockSpec((tm, 128), lambda i, t: (t, 0)),
                  pl.BlockSpec((8, tm), lambda i, t: (0, t)),
                  pl.BlockSpec((8, tm), lambda i, t: (0, t)),
                  _const_spec(carry0.shape), _const_spec(place.shape)],
        out_specs=(pl.BlockSpec((1, 1, N_QMAPS, 128, tm), lambda i, t: (i, t, 0, 0, 0)),
                   pl.BlockSpec((1, N_KBLOCKS, tm, 128), lambda i, t: (i, 0, t, 0)),
                   pl.BlockSpec((1, 1, DIFF_HEADS, VD_ROWS, tm), lambda i, t: (i, t, 0, 0, 0)),
                   pl.BlockSpec((1, 1, FOX_HEADS, VF_ROWS, tm), lambda i, t: (i, t, 0, 0, 0)),
                   pl.BlockSpec((1, 1, 8, 128), lambda i, t: (i, t, 0, 0))),
        out_shape=out_shape,
        scratch_shapes=[pltpu.VMEM((8, 128), F32)],
        compiler_params=pltpu.CompilerParams(dimension_semantics=("parallel", "arbitrary"),
                                             vmem_limit_bytes=VMEM_LIMIT),
        name="proj",
    )(h, g, wrows, wt, bfg, cc, ss, cost, sint, carry0, place)


def _attn_kernel(qi_ref, kg_ref, cend_ref,
                 qt_ref, k_ref, vd_ref, vf_ref, km_ref, vdm_ref, vfm_ref,
                 lq1_ref, lk1_ref, lq2_ref, lk2_ref, subg_ref,
                 o_ref, ref_a, max_a, accd_a, accf_a, ref_b, max_b, accd_b, accf_b,
                 outside_scr, par_scr, *, lambda_init, n_tiles):
    state_a = (ref_a, max_a, accd_a, accf_a)
    state_b = (ref_b, max_b, accd_b, accf_b)
    s_idx = pl.program_id(1)
    qi = qi_ref[s_idx]
    kg = kg_ref[s_idx]
    tq = qt_ref.shape[4]
    n_group = vd_ref.shape[1]
    tk = k_ref.shape[2] // n_group
    assert tq % QCOLS == 0 and tq == tk

    n_dmaps = 2 * DIFF_HEADS
    n_cb = tq // QCOLS
    units = [(u, cb) for u in range(N_QMAPS) for cb in range(n_cb)]

    def scores(unit, k_rows, n, mask):
        u, cb = unit
        kb = u // 2 if u < n_dmaps else u - DIFF_HEADS
        st = jnp.dot(k_rows(kb, n), qt_ref[0, 0, u, :, cb * QCOLS:(cb + 1) * QCOLS],
                     preferred_element_type=F32)
        if mask is not None:
            st = jnp.where(mask(cb), st, NEG_INF)
        return st, jnp.max(st, axis=0, keepdims=True)

    def update(unit, st_max, vd_cols, vf_cols, n, mode, src, dst, growth):
        u, cb = unit
        st, t_max = st_max
        idx = u * n_cb + cb
        cols = slice(cb * QCOLS, (cb + 1) * QCOLS)
        ref_dst, max_dst = dst[0], dst[1]
        if u < n_dmaps:
            group, a, v_t = 2, u, vd_cols(u // 2, n)
        else:
            group, a, v_t = 3, u - n_dmaps, vf_cols(u - n_dmaps, n)
        outside = None
        if mode == "first":
            shift = t_max
        else:
            r_old = src[0][idx]
            x_old = src[1][idx]
            if mode == "exact":
                shift = jnp.maximum(r_old, t_max)
            else:
                shift = x_old + growth[u - n_dmaps] if u >= n_dmaps else x_old
        p = jnp.exp2(st - shift).astype(BF16)
        pv = jnp.dot(v_t, p, preferred_element_type=F32)
        if mode == "first":
            dst[group][a, :, cols] = pv
            max_dst[idx] = t_max
        else:
            dst[group][a, :, cols] = src[group][a, :, cols] * jnp.exp2(r_old - shift) + pv
            x_new = jnp.maximum(x_old, t_max)
            max_dst[idx] = x_new
            if mode == "fast":
                outside = jnp.maximum(t_max - shift - FAST_OVER, shift - x_new - FAST_UNDER)
        ref_dst[idx] = shift
        return outside

    def sweep(k_rows, vd_cols, vf_cols, n_keys, mask, mode, src, dst, growth=None,
              lookahead=SCORE_LOOKAHEAD):
        pending = {w: scores(units[w], k_rows, n_keys(units[w][1]), mask)
                   for w in range(lookahead)}
        worst = None
        for w, unit in enumerate(units):
            nxt = w + lookahead
            if nxt < len(units):
                pending[nxt] = scores(units[nxt], k_rows, n_keys(units[nxt][1]), mask)
            outside = update(unit, pending.pop(w), vd_cols, vf_cols, n_keys(unit[1]),
                             mode, src, dst, growth)
            if outside is not None:
                worst = outside if worst is None else jnp.maximum(worst, outside)
        return worst

    def key_tile(g):
        row0 = pl.multiple_of(g * tk, tk)
        return (lambda p, n: k_ref[0, p, pl.ds(row0, n), :],
                lambda h, n: vd_ref[0, g, h, :, 0:n],
                lambda h, n: vf_ref[0, g, h, :, 0:n])

    @pl.when(kg == 0)
    def _():
        sweep(lambda p, n: km_ref[p], lambda h, n: vdm_ref[h], lambda h, n: vfm_ref[h],
              lambda cb: N_META, None, "first", None, state_a, lookahead=META_LOOKAHEAD)
        par_scr[0] = 0

    def copy_b_to_a():
        for dst_ref, src_ref in zip(state_a, state_b):
            for i in range(src_ref.shape[0]):
                dst_ref[i] = src_ref[i]

    first = kg * n_group
    cend_row = pl.program_id(0) * (n_tiles + 1) + first

    def full_tile(g, carry):
        par = par_scr[0]
        prev = (cend_row + g) * FOX_HEADS
        growth = [cend_ref[prev + hh] - cend_ref[prev + FOX_HEADS + hh] for hh in range(FOX_HEADS)]

        @pl.when(par == 0)
        def _():
            outside_scr[...] = sweep(*key_tile(g), lambda cb: tk, None, "fast",
                                     state_a, state_b, growth)

        @pl.when(par == 1)
        def _():
            outside_scr[...] = sweep(*key_tile(g), lambda cb: tk, None, "fast",
                                     state_b, state_a, growth)

        redo = jnp.max(outside_scr[...]) > 0.0

        @pl.when(redo)
        def _():
            @pl.when(par == 1)
            def _():
                copy_b_to_a()

            sweep(*key_tile(g), lambda cb: tk, None, "exact", state_a, state_b)

        par_scr[0] = jnp.where(redo, 1, 1 - par)
        return carry

    lax.fori_loop(0, jnp.clip(qi - first, 0, n_group), full_tile, 0)

    @pl.when(jnp.logical_and(qi >= first, qi < first + n_group))
    def _():
        def mask(cb):
            n = (cb + 1) * QCOLS
            key = lax.broadcasted_iota(jnp.int32, (n, QCOLS), 0)
            qry = lax.broadcasted_iota(jnp.int32, (n, QCOLS), 1)
            return key <= qry + cb * QCOLS

        @pl.when(par_scr[0] == 1)
        def _():
            copy_b_to_a()

        sweep(*key_tile(qi - first), lambda cb: (cb + 1) * QCOLS, mask, "exact", state_a, state_b)
        accd_scr, accf_scr = accd_b, accf_b

        lam = (jnp.exp(jnp.sum(lq1_ref[...] * lk1_ref[...], axis=1, keepdims=True))
               - jnp.exp(jnp.sum(lq2_ref[...] * lk2_ref[...], axis=1, keepdims=True))
               + lambda_init)
        for h in range(DIFF_HEADS):
            a1 = accd_scr[2 * h]
            a2 = accd_scr[2 * h + 1]
            o = (a1[0:DIFF_V_DIM] / a1[DIFF_V_DIM:DIFF_V_DIM + 1]
                 - lam * (a2[0:DIFF_V_DIM] / a2[DIFF_V_DIM:DIFF_V_DIM + 1]))
            ms = jnp.mean(o * o, axis=0, keepdims=True)
            o = o * lax.rsqrt(ms + SUBLN_EPS) * subg_ref[...] * (1.0 - lambda_init)
            o_ref[0, :, h * 128:(h + 1) * 128] = o.T.astype(o_ref.dtype)
        for i in range(FOX_HEADS // 2):
            a0 = accf_scr[2 * i]
            a1 = accf_scr[2 * i + 1]
            o = jnp.concatenate(
                [a0[0:FOX_HEAD_DIM] / a0[FOX_HEAD_DIM:FOX_HEAD_DIM + 1],
                 a1[0:FOX_HEAD_DIM] / a1[FOX_HEAD_DIM:FOX_HEAD_DIM + 1]], axis=0)
            c0 = DIFF_HEADS * DIFF_V_DIM + i * 128
            o_ref[0, :, c0:c0 + 128] = o.T.astype(o_ref.dtype)


def _attn(qt, k, vd, vf, km, vdm, vfm, cend, lq1, lk1, lq2, lk2, subg, lambda_init, tile):
    b, nq = qt.shape[0], qt.shape[1]
    s = nq * tile
    grp = min(KV_GROUP, nq)
    assert qt.shape[4] == tile and tile % QCOLS == 0 and nq % grp == 0
    qi_tab = np.concatenate([np.full(i // grp + 1, i, np.int32) for i in range(nq)])
    kg_tab = np.concatenate([np.arange(i // grp + 1, dtype=np.int32) for i in range(nq)])
    n_steps = int(qi_tab.shape[0])
    d_mix = DIFF_HEADS * DIFF_V_DIM + FOX_HEADS * FOX_HEAD_DIM

    def const(shape):
        nd = len(shape)
        return pl.BlockSpec(shape, lambda *_: (0,) * nd)

    n_units = N_QMAPS * (tile // QCOLS)
    grid_spec = pltpu.PrefetchScalarGridSpec(
        num_scalar_prefetch=3,
        grid=(b, n_steps),
        in_specs=[
            pl.BlockSpec((1, 1, N_QMAPS, 128, tile), lambda i, t, qi, kg, ce: (i, qi[t], 0, 0, 0)),
            pl.BlockSpec((1, N_KBLOCKS, grp * tile, 128), lambda i, t, qi, kg, ce: (i, 0, kg[t], 0)),
            pl.BlockSpec((1, grp, DIFF_HEADS, VD_ROWS, tile),
                         lambda i, t, qi, kg, ce: (i, kg[t], 0, 0, 0)),
            pl.BlockSpec((1, grp, FOX_HEADS, VF_ROWS, tile),
                         lambda i, t, qi, kg, ce: (i, kg[t], 0, 0, 0)),
            const(km.shape), const(vdm.shape), const(vfm.shape),
            const(lq1.shape), const(lk1.shape), const(lq2.shape), const(lk2.shape),
            const(subg.shape),
        ],
        out_specs=pl.BlockSpec((1, tile, d_mix), lambda i, t, qi, kg, ce: (i, qi[t], 0)),
        scratch_shapes=2 * [pltpu.VMEM((n_units, 1, QCOLS), F32),
                            pltpu.VMEM((n_units, 1, QCOLS), F32),
                            pltpu.VMEM((2 * DIFF_HEADS, VD_ROWS, tile), F32),
                            pltpu.VMEM((FOX_HEADS, VF_ROWS, tile), F32)]
        + [pltpu.VMEM((1, QCOLS), F32), pltpu.SMEM((1,), jnp.int32)],
    )
    return pl.pallas_call(
        functools.partial(_attn_kernel, lambda_init=lambda_init, n_tiles=nq),
        grid_spec=grid_spec,
        out_shape=jax.ShapeDtypeStruct((b, s, d_mix), BF16),
        compiler_params=pltpu.CompilerParams(dimension_semantics=("parallel", "arbitrary"),
                                             vmem_limit_bytes=VMEM_LIMIT),
        name="attn",
    )(jnp.asarray(qi_tab), jnp.asarray(kg_tab), cend.reshape(-1), qt, k, vd, vf, km, vdm, vfm,
      lq1, lk1, lq2, lk2, subg)


def _rope_swap_columns():
    n = 2 * DIFF_HEADS * DIFF_QK_DIM
    idx = np.arange(n)
    sgn = np.zeros(n, np.float32)
    r = idx % DIFF_QK_DIM
    half = ROPE_DIMS // 2
    first = r < half
    second = (r >= half) & (r < ROPE_DIMS)
    src = np.where(first, idx + half, np.where(second, idx - half, idx))
    sgn[first] = -1.0
    sgn[second] = 1.0
    return src.astype(np.int32), sgn


def _placement_matrix():
    m = np.zeros((3 * 128, FOX_HEADS * 128), np.float32)
    for p in range(3):
        for h in range(FOX_HEADS):
            m[p * 128 + h, h * 128 + FOX_HEAD_DIM + p] = 1.0
    return m


def kernel(x, meta_tokens, ffn1_norm_g, ffn1_w_gate_up, ffn1_w_down, mix_norm_g, w_in, b_forget, lam_q1, lam_k1, lam_q2, lam_k2, diff_subln_g, w_out, ffn2_norm_g, ffn2_w_gate_up, ffn2_w_down, final_norm_g):
    b, s, d = x.shape
    assert ffn1_norm_g.shape[0] == 1, "single-layer problem: meta rows are not carried past the mixer"
    assert s % ATTN_TILE == 0 and ROW_TILE == ATTN_TILE
    layer = 0
    lambda_init = _lambda_init(layer)

    row = lambda v: v.reshape(1, -1).astype(F32)
    wgu1 = ffn1_w_gate_up[layer].astype(BF16)
    wd1 = ffn1_w_down[layer].astype(BF16)
    wgu2 = ffn2_w_gate_up[layer].astype(BF16)
    wd2 = ffn2_w_down[layer].astype(BF16)
    wout = w_out[layer].astype(BF16)
    w = w_in[layer]
    n512 = 512
    dq_w, dk_w, dv_w, fq_w, fk_w, fv_w = (w[:, i * n512:(i + 1) * n512] for i in range(6))
    fl_w = w[:, 6 * n512:]
    src, sgn = _rope_swap_columns()
    dk_sw = jnp.take(dk_w, jnp.asarray(src), axis=1) * jnp.asarray(sgn)[None, :]
    fk_spread = jnp.pad(fk_w.reshape(d, FOX_HEADS, FOX_HEAD_DIM),
                        ((0, 0), (0, 0), (0, 128 - FOX_HEAD_DIM))).reshape(d, FOX_HEADS * 128)
    fl_blk = jnp.pad(fl_w, ((0, 0), (0, 128 - FOX_HEADS)))
    wrows = jnp.concatenate([dk_w, dk_sw, fk_spread, fl_blk], axis=1).astype(BF16)
    q_scale = QK_SCALE * LOG2E
    wt = jnp.concatenate([dq_w * q_scale, fq_w * q_scale, dv_w, fv_w], axis=1).T.astype(BF16)
    bfg = jnp.pad(b_forget[layer].reshape(1, -1).astype(F32), ((0, 0), (0, 128 - FOX_HEADS)))
    place = jnp.asarray(_placement_matrix()).astype(BF16)
    subg = diff_subln_g[layer].reshape(-1, 1).astype(F32)

    n_pos = max(N_META + s, META_TILE)
    pos = jnp.arange(n_pos, dtype=F32)
    inv_freq = jnp.power(ROPE_THETA, -jnp.arange(0, ROPE_DIMS, 2, dtype=F32) / ROPE_DIMS)
    ang = pos[:, None] * inv_freq[None, :]
    cos, sin = jnp.cos(ang), jnp.sin(ang)
    lane_r = np.arange(128) % DIFF_QK_DIM
    in_rope = jnp.asarray(lane_r < ROPE_DIMS)[None, :]
    lane_f = jnp.asarray(lane_r % (ROPE_DIMS // 2))
    cc_all = jnp.where(in_rope, jnp.take(cos, lane_f, axis=1), 1.0)
    ss_all = jnp.where(in_rope, jnp.take(sin, lane_f, axis=1), 0.0)
    cost_all, sint_all = cos.T, sin.T

    hm = jnp.pad(meta_tokens.astype(F32), ((0, META_TILE - N_META), (0, 0)))
    hm1 = _ffn1(hm, row(ffn1_norm_g[layer]), wgu1, wd1, META_TILE)
    zero_carry = jnp.zeros((8, 128), F32)
    _, km, vdm, vfm, carry_m = _proj(
        hm1[None], row(mix_norm_g[layer]), wrows, wt, bfg,
        cc_all[:META_TILE], ss_all[:META_TILE], cost_all[:, :META_TILE], sint_all[:, :META_TILE],
        zero_carry, place, META_TILE, N_META)

    h1 = _ffn1(x.reshape(b * s, d), row(ffn1_norm_g[layer]), wgu1, wd1, ROW_TILE)
    qt, k, vd, vf, cum_end = _proj(
        h1.reshape(b, s, d), row(mix_norm_g[layer]), wrows, wt, bfg,
        cc_all[N_META:N_META + s], ss_all[N_META:N_META + s],
        cost_all[:, N_META:N_META + s], sint_all[:, N_META:N_META + s],
        carry_m[0, 0], place, ROW_TILE, ROW_TILE)
    meta_end = jnp.broadcast_to(carry_m[0, 0, 0, :FOX_HEADS], (b, 1, FOX_HEADS))
    cend = jnp.concatenate([meta_end, cum_end[:, :, 0, :FOX_HEADS]], axis=1) * LOG2E
    o = _attn(qt, k, vd, vf, km[0, :, :N_META, :], vdm[0, 0, :, :, :N_META], vfm[0, 0, :, :, :N_META],
              cend, row(lam_q1[layer]), row(lam_k1[layer]), row(lam_q2[layer]), row(lam_k2[layer]),
              subg, lambda_init, ATTN_TILE)
    out = _tail(o.reshape(b * s, -1), h1, wout, row(ffn2_norm_g[layer]), wgu2, wd2,
                row(final_norm_g), ROW_TILE)
    return out.reshape(b, s, d)
```

```python
import functools
import math

import numpy as np
import jax
import jax.numpy as jnp
from jax import lax
from jax.experimental import pallas as pl
from jax.experimental.pallas import tpu as pltpu

F32 = jnp.float32
BF16 = jnp.bfloat16

N_META = 16
DIFF_HEADS = 4
DIFF_V_DIM = 128
DIFF_QK_DIM = 64
FOX_HEADS = 8
FOX_HEAD_DIM = 64
ROPE_DIMS = 16
ROPE_THETA = 500000.0
RMS_EPS = 1e-6
SUBLN_EPS = 1e-5
NEG_INF = -1e30
QK_SCALE = 0.125
LOG2E = math.log2(math.e)

N_QMAPS = 2 * DIFF_HEADS + FOX_HEADS
N_KBLOCKS = DIFF_HEADS + FOX_HEADS
BF16_SUBLANES = 16
VD_ROWS = DIFF_V_DIM + BF16_SUBLANES
VF_ROWS = FOX_HEAD_DIM + BF16_SUBLANES
META_TILE = 128
FFN_CHUNK = 256
ROW_TILE = 512
ATTN_TILE = 512
QCOLS = 256
FAST_OVER = 20.0
FAST_UNDER = 40.0
KV_GROUP = 4
SCORE_LOOKAHEAD = 3
DIAG_LOOKAHEAD = 5
META_LOOKAHEAD = 12
VMEM_LIMIT = 56 * 1024 * 1024


def _lambda_init(layer_idx):
    return 0.8 - 0.6 * math.exp(-0.3 * layer_idx)


def _rms(x, g, eps):
    ms = jnp.mean(x * x, axis=-1, keepdims=True)
    return x * lax.rsqrt(ms + eps) * g


def _split3(x):
    hi = x.astype(BF16)
    r = x - hi.astype(F32)
    mid = r.astype(BF16)
    lo = (r - mid.astype(F32)).astype(BF16)
    return hi, mid, lo


def _const_spec(shape):
    nd = len(shape)
    return pl.BlockSpec(shape, lambda *_: (0,) * nd, pipeline_mode=pl.Buffered(1))


def _swiglu_half_step(h, g_ref, wgu_ref, wd_ref):
    d_ff = wd_ref.shape[0]
    hn = _rms(h, g_ref[...], RMS_EPS).astype(BF16)
    acc = jnp.zeros(h.shape, F32)
    for c in range(d_ff // FFN_CHUNK):
        lo, hi = c * FFN_CHUNK, (c + 1) * FFN_CHUNK
        gate = jnp.dot(hn, wgu_ref[:, lo:hi], preferred_element_type=F32)
        up = jnp.dot(hn, wgu_ref[:, d_ff + lo:d_ff + hi], preferred_element_type=F32)
        act = (gate / (1.0 + jnp.exp(-gate)) * up).astype(BF16)
        acc = acc + jnp.dot(act, wd_ref[lo:hi, :], preferred_element_type=F32)
    return h + 0.5 * acc


def _ffn1_kernel(h_ref, g_ref, wgu_ref, wd_ref, o_ref):
    o_ref[...] = _swiglu_half_step(h_ref[...], g_ref, wgu_ref, wd_ref)


def _ffn1(h, g, wgu, wd, tm):
    rows, d = h.shape
    return pl.pallas_call(
        _ffn1_kernel,
        grid=(rows // tm,),
        in_specs=[pl.BlockSpec((tm, d), lambda i: (i, 0)),
                  _const_spec(g.shape), _const_spec(wgu.shape), _const_spec(wd.shape)],
        out_specs=pl.BlockSpec((tm, d), lambda i: (i, 0)),
        out_shape=jax.ShapeDtypeStruct((rows, d), F32),
        compiler_params=pltpu.CompilerParams(dimension_semantics=("parallel",),
                                             vmem_limit_bytes=VMEM_LIMIT),
        name="ffn1",
    )(h, g, wgu, wd)


def _tail_kernel(o_ref, h_ref, wout_ref, g_ref, wgu_ref, wd_ref, gf_ref, out_ref):
    h2 = h_ref[...] + jnp.dot(o_ref[...], wout_ref[...], preferred_element_type=F32)
    h3 = _swiglu_half_step(h2, g_ref, wgu_ref, wd_ref)
    out_ref[...] = _rms(h3, gf_ref[...], RMS_EPS)


def _tail(o, h1, wout, g, wgu, wd, gf, tm):
    rows, d = h1.shape
    return pl.pallas_call(
        _tail_kernel,
        grid=(rows // tm,),
        in_specs=[pl.BlockSpec((tm, d), lambda i: (i, 0)),
                  pl.BlockSpec((tm, d), lambda i: (i, 0)),
                  _const_spec(wout.shape), _const_spec(g.shape), _const_spec(wgu.shape),
                  _const_spec(wd.shape), _const_spec(gf.shape)],
        out_specs=pl.BlockSpec((tm, d), lambda i: (i, 0)),
        out_shape=jax.ShapeDtypeStruct((rows, d), F32),
        compiler_params=pltpu.CompilerParams(dimension_semantics=("parallel",),
                                             vmem_limit_bytes=VMEM_LIMIT),
        name="tail",
    )(o, h1, wout, g, wgu, wd, gf)


def _proj_kernel(h_ref, g_ref, wrows_ref, wt_ref, bf_ref, cc_ref, ss_ref, cost_ref, sint_ref,
                 carry0_ref, place_ref,
                 qt_ref, k_ref, vd_ref, vf_ref, cend_ref, carry_scr, *, n_valid):
    tm = h_ref.shape[1]

    @pl.when(pl.program_id(1) == 0)
    def _():
        carry_scr[...] = carry0_ref[...]

    hn = _rms(h_ref[0], g_ref[...], RMS_EPS).astype(BF16)
    kr = jnp.dot(hn, wrows_ref[...], preferred_element_type=F32)
    pt = lax.dot_general(wt_ref[...], hn, (((1,), (1,)), ((), ())),
                         preferred_element_type=F32)

    cc = cc_ref[...]
    ss = ss_ref[...]
    n_dk = DIFF_HEADS * 128
    for p in range(DIFF_HEADS):
        kd = kr[:, p * 128:(p + 1) * 128]
        ks = kr[:, n_dk + p * 128:n_dk + (p + 1) * 128]
        k_ref[0, p] = (kd * cc + ks * ss).astype(BF16)

    fk0 = 2 * n_dk
    fl0 = fk0 + FOX_HEADS * FOX_HEAD_DIM
    fl = kr[:, fl0:fl0 + 128] + bf_ref[...]
    logf = jnp.minimum(fl, 0.0) - jnp.log(1.0 + jnp.exp(-jnp.abs(fl)))
    lane = lax.broadcasted_iota(jnp.int32, (tm, 128), 1)
    keep = lane < FOX_HEADS
    if n_valid < tm:
        row = lax.broadcasted_iota(jnp.int32, (tm, 128), 0)
        keep = jnp.logical_and(keep, row < n_valid)
    logf = jnp.where(keep, logf, 0.0)
    parts = jnp.concatenate(_split3(logf), axis=1)
    r_i = lax.broadcasted_iota(jnp.int32, (tm, tm), 0)
    c_i = lax.broadcasted_iota(jnp.int32, (tm, tm), 1)
    tri = jnp.where(r_i >= c_i, 1.0, 0.0).astype(BF16)
    cs = jnp.dot(tri, parts, preferred_element_type=F32)
    cum = cs[:, 0:128] + cs[:, 128:256] + cs[:, 256:384] + carry_scr[0:1, :]
    carry_scr[...] = jnp.broadcast_to(cum[tm - 1:tm, :], carry_scr.shape)
    cend_ref[0, 0] = carry_scr[...]
    chi, cmid, clo = _split3(cum * LOG2E)
    cparts = (chi.astype(F32) + pltpu.roll(cmid.astype(F32), FOX_HEADS, 1)
              + pltpu.roll(clo.astype(F32), 2 * FOX_HEADS, 1)).astype(BF16)
    aug = jnp.dot(cparts, place_ref[...], preferred_element_type=F32)
    low_half = lane < FOX_HEAD_DIM
    for pr in range(FOX_HEADS // 2):
        pair = kr[:, fk0 + pr * 128:fk0 + (pr + 1) * 128]
        for c, blk in enumerate((pair, pltpu.roll(pair, FOX_HEAD_DIM, 1))):
            hh = 2 * pr + c
            kf = jnp.where(low_half, blk, aug[:, hh * 128:(hh + 1) * 128])
            k_ref[0, DIFF_HEADS + hh] = kf.astype(BF16)

    cost = cost_ref[...]
    sint = sint_ref[...]
    half = ROPE_DIMS // 2
    zeros64 = jnp.zeros((64, tm), BF16)
    for h in range(DIFF_HEADS):
        for c in range(2):
            base = h * 128 + c * 64
            x1 = pt[base:base + half]
            x2 = pt[base + half:base + ROPE_DIMS]
            blk = jnp.concatenate([x1 * cost - x2 * sint, x2 * cost + x1 * sint,
                                   pt[base + ROPE_DIMS:base + 64]], axis=0).astype(BF16)
            u = 2 * h + c
            qt_ref[0, 0, u, c * 64:(c + 1) * 64, :] = blk
            qt_ref[0, 0, u, (1 - c) * 64:(2 - c) * 64, :] = zeros64
    fq0 = DIFF_HEADS * 128
    rows64 = lax.broadcasted_iota(jnp.int32, (64, tm), 0)
    neg_rows = jnp.where(rows64 < 3, -1.0, 0.0).astype(BF16)
    for hh in range(FOX_HEADS):
        u = 2 * DIFF_HEADS + hh
        qt_ref[0, 0, u, 0:64, :] = pt[fq0 + hh * 64:fq0 + (hh + 1) * 64].astype(BF16)
        qt_ref[0, 0, u, 64:128, :] = neg_rows

    ones_rows = jnp.ones((BF16_SUBLANES, tm), BF16)
    dv0 = fq0 + FOX_HEADS * 64
    for h in range(DIFF_HEADS):
        vd_ref[0, 0, h, 0:DIFF_V_DIM, :] = pt[dv0 + h * 128:dv0 + (h + 1) * 128].astype(BF16)
        vd_ref[0, 0, h, DIFF_V_DIM:VD_ROWS, :] = ones_rows
    fv0 = dv0 + DIFF_HEADS * 128
    for hh in range(FOX_HEADS):
        vf_ref[0, 0, hh, 0:FOX_HEAD_DIM, :] = pt[fv0 + hh * 64:fv0 + (hh + 1) * 64].astype(BF16)
        vf_ref[0, 0, hh, FOX_HEAD_DIM:VF_ROWS, :] = ones_rows


def _proj(h, g, wrows, wt, bfg, cc, ss, cost, sint, carry0, place, tm, n_valid):
    b, s, d = h.shape
    nt = s // tm
    out_shape = (
        jax.ShapeDtypeStruct((b, nt, N_QMAPS, 128, tm), BF16),
        jax.ShapeDtypeStruct((b, N_KBLOCKS, s, 128), BF16),
        jax.ShapeDtypeStruct((b, nt, DIFF_HEADS, VD_ROWS, tm), BF16),
        jax.ShapeDtypeStruct((b, nt, FOX_HEADS, VF_ROWS, tm), BF16),
        jax.ShapeDtypeStruct((b, nt, 8, 128), F32),
    )
    return pl.pallas_call(
        functools.partial(_proj_kernel, n_valid=n_valid),
        grid=(b, nt),
        in_specs=[pl.BlockSpec((1, tm, d), lambda i, t: (i, t, 0)),
                  _const_spec(g.shape), _const_spec(wrows.shape), _const_spec(wt.shape),
                  _const_spec(bfg.shape),
                  pl.BlockSpec((tm, 128), lambda i, t: (t, 0)),
                  pl.BlockSpec((tm, 128), lambda i, t: (t, 0)),
                  pl.BlockSpec((8, tm), lambda i, t: (0, t)),
                  pl.BlockSpec((8, tm), lambda i, t: (0, t)),
                  _const_spec(carry0.shape), _const_spec(place.shape)],
        out_specs=(pl.BlockSpec((1, 1, N_QMAPS, 128, tm), lambda i, t: (i, t, 0, 0, 0)),
                   pl.BlockSpec((1, N_KBLOCKS, tm, 128), lambda i, t: (i, 0, t, 0)),
                   pl.BlockSpec((1, 1, DIFF_HEADS, VD_ROWS, tm), lambda i, t: (i, t, 0, 0, 0)),
                   pl.BlockSpec((1, 1, FOX_HEADS, VF_ROWS, tm), lambda i, t: (i, t, 0, 0, 0)),
                   pl.BlockSpec((1, 1, 8, 128), lambda i, t: (i, t, 0, 0))),
        out_shape=out_shape,
        scratch_shapes=[pltpu.VMEM((8, 128), F32)],
        compiler_params=pltpu.CompilerParams(dimension_semantics=("parallel", "arbitrary"),
                                             vmem_limit_bytes=VMEM_LIMIT),
        name="proj",
    )(h, g, wrows, wt, bfg, cc, ss, cost, sint, carry0, place)


def _attn_kernel(qi_ref, kg_ref, cend_ref,
                 qt_ref, k_ref, vd_ref, vf_ref, km_ref, vdm_ref, vfm_ref,
                 lq1_ref, lk1_ref, lq2_ref, lk2_ref, subg_ref,
                 o_ref, ref_a, max_a, accd_a, accf_a, ref_b, max_b, accd_b, accf_b,
                 outside_scr, par_scr, *, lambda_init, n_tiles):
    state_a = (ref_a, max_a, accd_a, accf_a)
    state_b = (ref_b, max_b, accd_b, accf_b)
    s_idx = pl.program_id(1)
    qi = qi_ref[s_idx]
    kg = kg_ref[s_idx]
    tq = qt_ref.shape[4]
    n_group = vd_ref.shape[1]
    tk = k_ref.shape[2] // n_group
    assert tq % QCOLS == 0 and tq == tk

    n_dmaps = 2 * DIFF_HEADS
    n_cb = tq // QCOLS
    units = [(u, cb) for u in range(N_QMAPS) for cb in range(n_cb)]

    def scores(unit, k_rows, n, mask):
        u, cb = unit
        kb = u // 2 if u < n_dmaps else u - DIFF_HEADS
        st = jnp.dot(k_rows(kb, n), qt_ref[0, 0, u, :, cb * QCOLS:(cb + 1) * QCOLS],
                     preferred_element_type=F32)
        if mask is not None:
            st = jnp.where(mask(cb), st, NEG_INF)
        return st, jnp.max(st, axis=0, keepdims=True)

    def update(unit, st_max, vd_cols, vf_cols, n, mode, src, dst, growth):
        u, cb = unit
        st, t_max = st_max
        idx = u * n_cb + cb
        cols = slice(cb * QCOLS, (cb + 1) * QCOLS)
        ref_dst, max_dst = dst[0], dst[1]
        if u < n_dmaps:
            group, a, v_t = 2, u, vd_cols(u // 2, n)
        else:
            group, a, v_t = 3, u - n_dmaps, vf_cols(u - n_dmaps, n)
        outside = None
        if mode == "first":
            shift = t_max
        else:
            r_old = src[0][idx]
            if mode == "fast":
                x_old = src[1][idx]
                shift = x_old + growth[u - n_dmaps] if u >= n_dmaps else x_old
            else:
                shift = jnp.maximum(r_old, t_max)
        p = jnp.exp2(st - shift).astype(BF16)
        pv = jnp.dot(v_t, p, preferred_element_type=F32)
        if mode == "first":
            dst[group][a, :, cols] = pv
            max_dst[idx] = t_max
        else:
            dst[group][a, :, cols] = src[group][a, :, cols] * jnp.exp2(r_old - shift) + pv
            if mode == "fast":
                x_new = jnp.maximum(x_old, t_max)
                max_dst[idx] = x_new
                outside = jnp.maximum(t_max - shift - FAST_OVER, shift - x_new - FAST_UNDER)
            elif mode == "exact":
                max_dst[idx] = jnp.maximum(src[1][idx], t_max)
        ref_dst[idx] = shift
        return outside

    def sweep(k_rows, vd_cols, vf_cols, n_keys, mask, mode, src, dst, growth=None,
              lookahead=SCORE_LOOKAHEAD):
        pending = {w: scores(units[w], k_rows, n_keys(units[w][1]), mask)
                   for w in range(lookahead)}
        worst = None
        for w, unit in enumerate(units):
            nxt = w + lookahead
            if nxt < len(units):
                pending[nxt] = scores(units[nxt], k_rows, n_keys(units[nxt][1]), mask)
            outside = update(unit, pending.pop(w), vd_cols, vf_cols, n_keys(unit[1]),
                             mode, src, dst, growth)
            if outside is not None:
                worst = outside if worst is None else jnp.maximum(worst, outside)
        return worst

    def key_tile(g):
        row0 = pl.multiple_of(g * tk, tk)
        return (lambda p, n: k_ref[0, p, pl.ds(row0, n), :],
                lambda h, n: vd_ref[0, g, h, :, 0:n],
                lambda h, n: vf_ref[0, g, h, :, 0:n])

    @pl.when(kg == 0)
    def _():
        sweep(lambda p, n: km_ref[p], lambda h, n: vdm_ref[h], lambda h, n: vfm_ref[h],
              lambda cb: N_META, None, "first", None, state_a, lookahead=META_LOOKAHEAD)
        par_scr[0] = 0

    def copy_b_to_a():
        for dst_ref, src_ref in zip(state_a, state_b):
            for i in range(src_ref.shape[0]):
                dst_ref[i] = src_ref[i]

    first = kg * n_group
    cend_row = pl.program_id(0) * (n_tiles + 1) + first

    def full_tile(g, carry):
        par = par_scr[0]
        prev = (cend_row + g) * FOX_HEADS
        growth = [cend_ref[prev + hh] - cend_ref[prev + FOX_HEADS + hh] for hh in range(FOX_HEADS)]

        @pl.when(par == 0)
        def _():
            outside_scr[...] = sweep(*key_tile(g), lambda cb: tk, None, "fast",
                                     state_a, state_b, growth)

        @pl.when(par == 1)
        def _():
            outside_scr[...] = sweep(*key_tile(g), lambda cb: tk, None, "fast",
                                     state_b, state_a, growth)

        redo = jnp.max(outside_scr[...]) > 0.0

        @pl.when(redo)
        def _():
            @pl.when(par == 1)
            def _():
                copy_b_to_a()

            sweep(*key_tile(g), lambda cb: tk, None, "exact", state_a, state_b)

        par_scr[0] = jnp.where(redo, 1, 1 - par)
        return carry

    lax.fori_loop(0, jnp.clip(qi - first, 0, n_group), full_tile, 0)

    @pl.when(jnp.logical_and(qi >= first, qi < first + n_group))
    def _():
        def mask(cb):
            n = (cb + 1) * QCOLS
            key = lax.broadcasted_iota(jnp.int32, (n, QCOLS), 0)
            qry = lax.broadcasted_iota(jnp.int32, (n, QCOLS), 1)
            return key <= qry + cb * QCOLS

        @pl.when(par_scr[0] == 1)
        def _():
            copy_b_to_a()

        sweep(*key_tile(qi - first), lambda cb: (cb + 1) * QCOLS, mask, "last", state_a, state_a,
              lookahead=DIAG_LOOKAHEAD)
        accd_scr, accf_scr = accd_a, accf_a

        lam = (jnp.exp(jnp.sum(lq1_ref[...] * lk1_ref[...], axis=1, keepdims=True))
               - jnp.exp(jnp.sum(lq2_ref[...] * lk2_ref[...], axis=1, keepdims=True))
               + lambda_init)
        for h in range(DIFF_HEADS):
            a1 = accd_scr[2 * h]
            a2 = accd_scr[2 * h + 1]
            o = (a1[0:DIFF_V_DIM] / a1[DIFF_V_DIM:DIFF_V_DIM + 1]
                 - lam * (a2[0:DIFF_V_DIM] / a2[DIFF_V_DIM:DIFF_V_DIM + 1]))
            ms = jnp.mean(o * o, axis=0, keepdims=True)
            o = o * lax.rsqrt(ms + SUBLN_EPS) * subg_ref[...] * (1.0 - lambda_init)
            o_ref[0, :, h * 128:(h + 1) * 128] = o.T.astype(o_ref.dtype)
        for i in range(FOX_HEADS // 2):
            a0 = accf_scr[2 * i]
            a1 = accf_scr[2 * i + 1]
            o = jnp.concatenate(
                [a0[0:FOX_HEAD_DIM] / a0[FOX_HEAD_DIM:FOX_HEAD_DIM + 1],
                 a1[0:FOX_HEAD_DIM] / a1[FOX_HEAD_DIM:FOX_HEAD_DIM + 1]], axis=0)
            c0 = DIFF_HEADS * DIFF_V_DIM + i * 128
            o_ref[0, :, c0:c0 + 128] = o.T.astype(o_ref.dtype)


def _attn(qt, k, vd, vf, km, vdm, vfm, cend, lq1, lk1, lq2, lk2, subg, lambda_init, tile):
    b, nq = qt.shape[0], qt.shape[1]
    s = nq * tile
    grp = min(KV_GROUP, nq)
    assert qt.shape[4] == tile and tile % QCOLS == 0 and nq % grp == 0
    qi_tab = np.concatenate([np.full(i // grp + 1, i, np.int32) for i in range(nq)])
    kg_tab = np.concatenate([np.arange(i // grp + 1, dtype=np.int32) for i in range(nq)])
    n_steps = int(qi_tab.shape[0])
    d_mix = DIFF_HEADS * DIFF_V_DIM + FOX_HEADS * FOX_HEAD_DIM

    def const(shape):
        nd = len(shape)
        return pl.BlockSpec(shape, lambda *_: (0,) * nd)

    n_units = N_QMAPS * (tile // QCOLS)
    grid_spec = pltpu.PrefetchScalarGridSpec(
        num_scalar_prefetch=3,
        grid=(b, n_steps),
        in_specs=[
            pl.BlockSpec((1, 1, N_QMAPS, 128, tile), lambda i, t, qi, kg, ce: (i, qi[t], 0, 0, 0)),
            pl.BlockSpec((1, N_KBLOCKS, grp * tile, 128), lambda i, t, qi, kg, ce: (i, 0, kg[t], 0)),
            pl.BlockSpec((1, grp, DIFF_HEADS, VD_ROWS, tile),
                         lambda i, t, qi, kg, ce: (i, kg[t], 0, 0, 0)),
            pl.BlockSpec((1, grp, FOX_HEADS, VF_ROWS, tile),
                         lambda i, t, qi, kg, ce: (i, kg[t], 0, 0, 0)),
            const(km.shape), const(vdm.shape), const(vfm.shape),
            const(lq1.shape), const(lk1.shape), const(lq2.shape), const(lk2.shape),
            const(subg.shape),
        ],
        out_specs=pl.BlockSpec((1, tile, d_mix), lambda i, t, qi, kg, ce: (i, qi[t], 0)),
        scratch_shapes=2 * [pltpu.VMEM((n_units, 1, QCOLS), F32),
                            pltpu.VMEM((n_units, 1, QCOLS), F32),
                            pltpu.VMEM((2 * DIFF_HEADS, VD_ROWS, tile), F32),
                            pltpu.VMEM((FOX_HEADS, VF_ROWS, tile), F32)]
        + [pltpu.VMEM((1, QCOLS), F32), pltpu.SMEM((1,), jnp.int32)],
    )
    return pl.pallas_call(
        functools.partial(_attn_kernel, lambda_init=lambda_init, n_tiles=nq),
        grid_spec=grid_spec,
        out_shape=jax.ShapeDtypeStruct((b, s, d_mix), BF16),
        compiler_params=pltpu.CompilerParams(dimension_semantics=("parallel", "arbitrary"),
                                             vmem_limit_bytes=VMEM_LIMIT),
        name="attn",
    )(jnp.asarray(qi_tab), jnp.asarray(kg_tab), cend.reshape(-1), qt, k, vd, vf, km, vdm, vfm,
      lq1, lk1, lq2, lk2, subg)


def _rope_swap_columns():
    n = 2 * DIFF_HEADS * DIFF_QK_DIM
    idx = np.arange(n)
    sgn = np.zeros(n, np.float32)
    r = idx % DIFF_QK_DIM
    half = ROPE_DIMS // 2
    first = r < half
    second = (r >= half) & (r < ROPE_DIMS)
    src = np.where(first, idx + half, np.where(second, idx - half, idx))
    sgn[first] = -1.0
    sgn[second] = 1.0
    return src.astype(np.int32), sgn


def _placement_matrix():
    m = np.zeros((128, FOX_HEADS * 128), np.float32)
    for p in range(3):
        for h in range(FOX_HEADS):
            m[p * FOX_HEADS + h, h * 128 + FOX_HEAD_DIM + p] = 1.0
    return m


def kernel(x, meta_tokens, ffn1_norm_g, ffn1_w_gate_up, ffn1_w_down, mix_norm_g, w_in, b_forget, lam_q1, lam_k1, lam_q2, lam_k2, diff_subln_g, w_out, ffn2_norm_g, ffn2_w_gate_up, ffn2_w_down, final_norm_g):
    b, s, d = x.shape
    assert ffn1_norm_g.shape[0] == 1, "single-layer problem: meta rows are not carried past the mixer"
    assert s % ATTN_TILE == 0 and ROW_TILE == ATTN_TILE
    layer = 0
    lambda_init = _lambda_init(layer)

    row = lambda v: v.reshape(1, -1).astype(F32)
    wgu1 = ffn1_w_gate_up[layer].astype(BF16)
    wd1 = ffn1_w_down[layer].astype(BF16)
    wgu2 = ffn2_w_gate_up[layer].astype(BF16)
    wd2 = ffn2_w_down[layer].astype(BF16)
    wout = w_out[layer].astype(BF16)
    w = w_in[layer]
    n512 = 512
    dq_w, dk_w, dv_w, fq_w, fk_w, fv_w = (w[:, i * n512:(i + 1) * n512] for i in range(6))
    fl_w = w[:, 6 * n512:]
    src, sgn = _rope_swap_columns()
    dk_sw = jnp.take(dk_w, jnp.asarray(src), axis=1) * jnp.asarray(sgn)[None, :]
    fl_blk = jnp.pad(fl_w, ((0, 0), (0, 128 - FOX_HEADS)))
    wrows = jnp.concatenate([dk_w, dk_sw, fk_w, fl_blk], axis=1).astype(BF16)
    q_scale = QK_SCALE * LOG2E
    wt = jnp.concatenate([dq_w * q_scale, fq_w * q_scale, dv_w, fv_w], axis=1).T.astype(BF16)
    bfg = jnp.pad(b_forget[layer].reshape(1, -1).astype(F32), ((0, 0), (0, 128 - FOX_HEADS)))
    place = jnp.asarray(_placement_matrix()).astype(BF16)
    subg = diff_subln_g[layer].reshape(-1, 1).astype(F32)

    n_pos = max(N_META + s, META_TILE)
    pos = jnp.arange(n_pos, dtype=F32)
    inv_freq = jnp.power(ROPE_THETA, -jnp.arange(0, ROPE_DIMS, 2, dtype=F32) / ROPE_DIMS)
    ang = pos[:, None] * inv_freq[None, :]
    cos, sin = jnp.cos(ang), jnp.sin(ang)
    lane_r = np.arange(128) % DIFF_QK_DIM
    in_rope = jnp.asarray(lane_r < ROPE_DIMS)[None, :]
    lane_f = jnp.asarray(lane_r % (ROPE_DIMS // 2))
    cc_all = jnp.where(in_rope, jnp.take(cos, lane_f, axis=1), 1.0)
    ss_all = jnp.where(in_rope, jnp.take(sin, lane_f, axis=1), 0.0)
    cost_all, sint_all = cos.T, sin.T

    hm = jnp.pad(meta_tokens.astype(F32), ((0, META_TILE - N_META), (0, 0)))
    hm1 = _ffn1(hm, row(ffn1_norm_g[layer]), wgu1, wd1, META_TILE)
    zero_carry = jnp.zeros((8, 128), F32)
    _, km, vdm, vfm, carry_m = _proj(
        hm1[None], row(mix_norm_g[layer]), wrows, wt, bfg,
        cc_all[:META_TILE], ss_all[:META_TILE], cost_all[:, :META_TILE], sint_all[:, :META_TILE],
        zero_carry, place, META_TILE, N_META)

    h1 = _ffn1(x.reshape(b * s, d), row(ffn1_norm_g[layer]), wgu1, wd1, ROW_TILE)
    qt, k, vd, vf, cum_end = _proj(
        h1.reshape(b, s, d), row(mix_norm_g[layer]), wrows, wt, bfg,
        cc_all[N_META:N_META + s], ss_all[N_META:N_META + s],
        cost_all[:, N_META:N_META + s], sint_all[:, N_META:N_META + s],
        carry_m[0, 0], place, ROW_TILE, ROW_TILE)
    meta_end = jnp.broadcast_to(carry_m[0, 0, 0, :FOX_HEADS], (b, 1, FOX_HEADS))
    cend = jnp.concatenate([meta_end, cum_end[:, :, 0, :FOX_HEADS]], axis=1) * LOG2E
    o = _attn(qt, k, vd, vf, km[0, :, :N_META, :], vdm[0, 0, :, :, :N_META], vfm[0, 0, :, :, :N_META],
              cend, row(lam_q1[layer]), row(lam_k1[layer]), row(lam_q2[layer]), row(lam_k2[layer]),
              subg, lambda_init, ATTN_TILE)
    out = _tail(o.reshape(b * s, -1), h1, wout, row(ffn2_norm_g[layer]), wgu2, wd2,
                row(final_norm_g), ROW_TILE)
    return out.reshape(b, s, d)
```

```python
import functools
import math

import numpy as np
import jax
import jax.numpy as jnp
from jax import lax
from jax.experimental import pallas as pl
from jax.experimental.pallas import tpu as pltpu

F32 = jnp.float32
BF16 = jnp.bfloat16

N_META = 16
DIFF_HEADS = 4
DIFF_V_DIM = 128
DIFF_QK_DIM = 64
FOX_HEADS = 8
FOX_HEAD_DIM = 64
ROPE_DIMS = 16
ROPE_THETA = 500000.0
RMS_EPS = 1e-6
SUBLN_EPS = 1e-5
NEG_INF = -1e30
QK_SCALE = 0.125
LOG2E = math.log2(math.e)

N_QMAPS = 2 * DIFF_HEADS + FOX_HEADS
N_KBLOCKS = DIFF_HEADS + FOX_HEADS
BF16_SUBLANES = 16
VD_ROWS = DIFF_V_DIM + BF16_SUBLANES
VF_ROWS = FOX_HEAD_DIM + BF16_SUBLANES
META_TILE = 128
FFN_CHUNK = 256
ROW_TILE = 512
FFN_TILE = 1024
ATTN_TILE = 512
QCOLS = 256
FAST_OVER = 20.0
FAST_UNDER = 40.0
KV_GROUP = 4
SCORE_LOOKAHEAD = 3
DIAG_LOOKAHEAD = 5
META_LOOKAHEAD = 12
VMEM_LIMIT = 56 * 1024 * 1024


def _lambda_init(layer_idx):
    return 0.8 - 0.6 * math.exp(-0.3 * layer_idx)


def _rms(x, g, eps):
    ms = jnp.mean(x * x, axis=-1, keepdims=True)
    return x * lax.rsqrt(ms + eps) * g


def _split3(x):
    hi = x.astype(BF16)
    r = x - hi.astype(F32)
    mid = r.astype(BF16)
    lo = (r - mid.astype(F32)).astype(BF16)
    return hi, mid, lo


def _const_spec(shape):
    nd = len(shape)
    return pl.BlockSpec(shape, lambda *_: (0,) * nd, pipeline_mode=pl.Buffered(1))


def _swiglu_half_step(h, g_ref, wgu_ref, wd_ref):
    d_ff = wd_ref.shape[0]
    hn = _rms(h, g_ref[...], RMS_EPS).astype(BF16)
    acc = jnp.zeros(h.shape, F32)
    for c in range(d_ff // FFN_CHUNK):
        lo, hi = c * FFN_CHUNK, (c + 1) * FFN_CHUNK
        gate = jnp.dot(hn, wgu_ref[:, lo:hi], preferred_element_type=F32)
        up = jnp.dot(hn, wgu_ref[:, d_ff + lo:d_ff + hi], preferred_element_type=F32)
        act = (gate / (1.0 + jnp.exp(-gate)) * up).astype(BF16)
        acc = acc + jnp.dot(act, wd_ref[lo:hi, :], preferred_element_type=F32)
    return h + 0.5 * acc


def _ffn1_kernel(h_ref, g_ref, wgu_ref, wd_ref, o_ref):
    o_ref[...] = _swiglu_half_step(h_ref[...], g_ref, wgu_ref, wd_ref)


def _ffn1(h, g, wgu, wd, tm):
    rows, d = h.shape
    return pl.pallas_call(
        _ffn1_kernel,
        grid=(rows // tm,),
        in_specs=[pl.BlockSpec((tm, d), lambda i: (i, 0)),
                  _const_spec(g.shape), _const_spec(wgu.shape), _const_spec(wd.shape)],
        out_specs=pl.BlockSpec((tm, d), lambda i: (i, 0)),
        out_shape=jax.ShapeDtypeStruct((rows, d), F32),
        compiler_params=pltpu.CompilerParams(dimension_semantics=("parallel",),
                                             vmem_limit_bytes=VMEM_LIMIT),
        name="ffn1",
    )(h, g, wgu, wd)


def _tail_kernel(o_ref, h_ref, wout_ref, g_ref, wgu_ref, wd_ref, gf_ref, out_ref):
    h2 = h_ref[...] + jnp.dot(o_ref[...], wout_ref[...], preferred_element_type=F32)
    h3 = _swiglu_half_step(h2, g_ref, wgu_ref, wd_ref)
    out_ref[...] = _rms(h3, gf_ref[...], RMS_EPS)


def _tail(o, h1, wout, g, wgu, wd, gf, tm):
    rows, d = h1.shape
    return pl.pallas_call(
        _tail_kernel,
        grid=(rows // tm,),
        in_specs=[pl.BlockSpec((tm, d), lambda i: (i, 0)),
                  pl.BlockSpec((tm, d), lambda i: (i, 0)),
                  _const_spec(wout.shape), _const_spec(g.shape), _const_spec(wgu.shape),
                  _const_spec(wd.shape), _const_spec(gf.shape)],
        out_specs=pl.BlockSpec((tm, d), lambda i: (i, 0)),
        out_shape=jax.ShapeDtypeStruct((rows, d), F32),
        compiler_params=pltpu.CompilerParams(dimension_semantics=("parallel",),
                                             vmem_limit_bytes=VMEM_LIMIT),
        name="tail",
    )(o, h1, wout, g, wgu, wd, gf)


def _proj_kernel(h_ref, g_ref, wrows_ref, wt_ref, bf_ref, cc_ref, ss_ref, cost_ref, sint_ref,
                 carry0_ref, place_ref,
                 qt_ref, k_ref, vd_ref, vf_ref, cend_ref, carry_scr, *, n_valid):
    tm = h_ref.shape[1]

    @pl.when(pl.program_id(1) == 0)
    def _():
        carry_scr[...] = carry0_ref[...]

    hn = _rms(h_ref[0], g_ref[...], RMS_EPS).astype(BF16)
    kr = jnp.dot(hn, wrows_ref[...], preferred_element_type=F32)
    pt = lax.dot_general(wt_ref[...], hn, (((1,), (1,)), ((), ())),
                         preferred_element_type=F32)

    cc = cc_ref[...]
    ss = ss_ref[...]
    n_dk = DIFF_HEADS * 128
    for p in range(DIFF_HEADS):
        kd = kr[:, p * 128:(p + 1) * 128]
        ks = kr[:, n_dk + p * 128:n_dk + (p + 1) * 128]
        k_ref[0, p] = (kd * cc + ks * ss).astype(BF16)

    fk0 = 2 * n_dk
    fl0 = fk0 + FOX_HEADS * FOX_HEAD_DIM
    fl = kr[:, fl0:fl0 + 128] + bf_ref[...]
    logf = jnp.minimum(fl, 0.0) - jnp.log(1.0 + jnp.exp(-jnp.abs(fl)))
    lane = lax.broadcasted_iota(jnp.int32, (tm, 128), 1)
    keep = lane < FOX_HEADS
    if n_valid < tm:
        row = lax.broadcasted_iota(jnp.int32, (tm, 128), 0)
        keep = jnp.logical_and(keep, row < n_valid)
    logf = jnp.where(keep, logf, 0.0)
    parts = jnp.concatenate(_split3(logf), axis=1)
    r_i = lax.broadcasted_iota(jnp.int32, (tm, tm), 0)
    c_i = lax.broadcasted_iota(jnp.int32, (tm, tm), 1)
    tri = jnp.where(r_i >= c_i, 1.0, 0.0).astype(BF16)
    cs = jnp.dot(tri, parts, preferred_element_type=F32)
    cum = cs[:, 0:128] + cs[:, 128:256] + cs[:, 256:384] + carry_scr[0:1, :]
    carry_scr[...] = jnp.broadcast_to(cum[tm - 1:tm, :], carry_scr.shape)
    cend_ref[0, 0] = carry_scr[...]
    chi, cmid, clo = _split3(cum * LOG2E)
    cparts = (chi.astype(F32) + pltpu.roll(cmid.astype(F32), FOX_HEADS, 1)
              + pltpu.roll(clo.astype(F32), 2 * FOX_HEADS, 1)).astype(BF16)
    aug = jnp.dot(cparts, place_ref[...], preferred_element_type=F32)
    low_half = lane < FOX_HEAD_DIM
    for pr in range(FOX_HEADS // 2):
        pair = kr[:, fk0 + pr * 128:fk0 + (pr + 1) * 128]
        for c, blk in enumerate((pair, pltpu.roll(pair, FOX_HEAD_DIM, 1))):
            hh = 2 * pr + c
            kf = jnp.where(low_half, blk, aug[:, hh * 128:(hh + 1) * 128])
            k_ref[0, DIFF_HEADS + hh] = kf.astype(BF16)

    cost = cost_ref[...]
    sint = sint_ref[...]
    half = ROPE_DIMS // 2
    zeros64 = jnp.zeros((64, tm), BF16)
    for h in range(DIFF_HEADS):
        for c in range(2):
            base = h * 128 + c * 64
            x1 = pt[base:base + half]
            x2 = pt[base + half:base + ROPE_DIMS]
            blk = jnp.concatenate([x1 * cost - x2 * sint, x2 * cost + x1 * sint,
                                   pt[base + ROPE_DIMS:base + 64]], axis=0).astype(BF16)
            u = 2 * h + c
            qt_ref[0, 0, u, c * 64:(c + 1) * 64, :] = blk
            qt_ref[0, 0, u, (1 - c) * 64:(2 - c) * 64, :] = zeros64
    fq0 = DIFF_HEADS * 128
    rows64 = lax.broadcasted_iota(jnp.int32, (64, tm), 0)
    neg_rows = jnp.where(rows64 < 3, -1.0, 0.0).astype(BF16)
    for hh in range(FOX_HEADS):
        u = 2 * DIFF_HEADS + hh
        qt_ref[0, 0, u, 0:64, :] = pt[fq0 + hh * 64:fq0 + (hh + 1) * 64].astype(BF16)
        qt_ref[0, 0, u, 64:128, :] = neg_rows

    ones_rows = jnp.ones((BF16_SUBLANES, tm), BF16)
    dv0 = fq0 + FOX_HEADS * 64
    for h in range(DIFF_HEADS):
        vd_ref[0, 0, h, 0:DIFF_V_DIM, :] = pt[dv0 + h * 128:dv0 + (h + 1) * 128].astype(BF16)
        vd_ref[0, 0, h, DIFF_V_DIM:VD_ROWS, :] = ones_rows
    fv0 = dv0 + DIFF_HEADS * 128
    for hh in range(FOX_HEADS):
        vf_ref[0, 0, hh, 0:FOX_HEAD_DIM, :] = pt[fv0 + hh * 64:fv0 + (hh + 1) * 64].astype(BF16)
        vf_ref[0, 0, hh, FOX_HEAD_DIM:VF_ROWS, :] = ones_rows


def _proj(h, g, wrows, wt, bfg, cc, ss, cost, sint, carry0, place, tm, n_valid):
    b, s, d = h.shape
    nt = s // tm
    out_shape = (
        jax.ShapeDtypeStruct((b, nt, N_QMAPS, 128, tm), BF16),
        jax.ShapeDtypeStruct((b, N_KBLOCKS, s, 128), BF16),
        jax.ShapeDtypeStruct((b, nt, DIFF_HEADS, VD_ROWS, tm), BF16),
        jax.ShapeDtypeStruct((b, nt, FOX_HEADS, VF_ROWS, tm), BF16),
        jax.ShapeDtypeStruct((b, nt, 8, 128), F32),
    )
    return pl.pallas_call(
        functools.partial(_proj_kernel, n_valid=n_valid),
        grid=(b, nt),
        in_specs=[pl.BlockSpec((1, tm, d), lambda i, t: (i, t, 0)),
                  _const_spec(g.shape), _const_spec(wrows.shape), _const_spec(wt.shape),
                  _const_spec(bfg.shape),
                  pl.BlockSpec((tm, 128), lambda i, t: (t, 0)),
                  pl.BlockSpec((tm, 128), lambda i, t: (t, 0)),
                  pl.BlockSpec((8, tm), lambda i, t: (0, t)),
                  pl.BlockSpec((8, tm), lambda i, t: (0, t)),
                  _const_spec(carry0.shape), _const_spec(place.shape)],
        out_specs=(pl.BlockSpec((1, 1, N_QMAPS, 128, tm), lambda i, t: (i, t, 0, 0, 0)),
                   pl.BlockSpec((1, N_KBLOCKS, tm, 128), lambda i, t: (i, 0, t, 0)),
                   pl.BlockSpec((1, 1, DIFF_HEADS, VD_ROWS, tm), lambda i, t: (i, t, 0, 0, 0)),
                   pl.BlockSpec((1, 1, FOX_HEADS, VF_ROWS, tm), lambda i, t: (i, t, 0, 0, 0)),
                   pl.BlockSpec((1, 1, 8, 128), lambda i, t: (i, t, 0, 0))),
        out_shape=out_shape,
        scratch_shapes=[pltpu.VMEM((8, 128), F32)],
        compiler_params=pltpu.CompilerParams(dimension_semantics=("parallel", "arbitrary"),
                                             vmem_limit_bytes=VMEM_LIMIT),
        name="proj",
    )(h, g, wrows, wt, bfg, cc, ss, cost, sint, carry0, place)


def _attn_kernel(qi_ref, kg_ref, cend_ref,
                 qt_ref, k_ref, vd_ref, vf_ref, km_ref, vdm_ref, vfm_ref,
                 lq1_ref, lk1_ref, lq2_ref, lk2_ref, subg_ref,
                 o_ref, ref_a, max_a, accd_a, accf_a, ref_b, max_b, accd_b, accf_b,
                 outside_scr, par_scr, *, lambda_init, n_tiles):
    state_a = (ref_a, max_a, accd_a, accf_a)
    state_b = (ref_b, max_b, accd_b, accf_b)
    s_idx = pl.program_id(1)
    qi = qi_ref[s_idx]
    kg = kg_ref[s_idx]
    tq = qt_ref.shape[4]
    n_group = vd_ref.shape[1]
    tk = k_ref.shape[2] // n_group
    assert tq % QCOLS == 0 and tq == tk

    n_dmaps = 2 * DIFF_HEADS
    n_cb = tq // QCOLS
    units = [(u, cb) for u in range(N_QMAPS) for cb in range(n_cb)]

    def scores(unit, k_rows, n, mask):
        u, cb = unit
        kb = u // 2 if u < n_dmaps else u - DIFF_HEADS
        st = jnp.dot(k_rows(kb, n), qt_ref[0, 0, u, :, cb * QCOLS:(cb + 1) * QCOLS],
                     preferred_element_type=F32)
        if mask is not None:
            st = jnp.where(mask(cb), st, NEG_INF)
        return st, jnp.max(st, axis=0, keepdims=True)

    def update(unit, st_max, vd_cols, vf_cols, n, mode, src, dst, growth):
        u, cb = unit
        st, t_max = st_max
        idx = u * n_cb + cb
        cols = slice(cb * QCOLS, (cb + 1) * QCOLS)
        ref_dst, max_dst = dst[0], dst[1]
        if u < n_dmaps:
            group, a, v_t = 2, u, vd_cols(u // 2, n)
        else:
            group, a, v_t = 3, u - n_dmaps, vf_cols(u - n_dmaps, n)
        outside = None
        if mode == "first":
            shift = t_max
        else:
            r_old = src[0][idx]
            if mode == "fast":
                x_old = src[1][idx]
                shift = x_old + growth[u - n_dmaps] if u >= n_dmaps else x_old
            else:
                shift = jnp.maximum(r_old, t_max)
        p = jnp.exp2(st - shift).astype(BF16)
        pv = jnp.dot(v_t, p, preferred_element_type=F32)
        if mode == "first":
            dst[group][a, :, cols] = pv
            max_dst[idx] = t_max
        else:
            dst[group][a, :, cols] = src[group][a, :, cols] * jnp.exp2(r_old - shift) + pv
            if mode == "fast":
                x_new = jnp.maximum(x_old, t_max)
                max_dst[idx] = x_new
                outside = jnp.maximum(t_max - shift - FAST_OVER, shift - x_new - FAST_UNDER)
            elif mode == "exact":
                max_dst[idx] = jnp.maximum(src[1][idx], t_max)
        ref_dst[idx] = shift
        return outside

    def sweep(k_rows, vd_cols, vf_cols, n_keys, mask, mode, src, dst, growth=None,
              lookahead=SCORE_LOOKAHEAD):
        pending = {w: scores(units[w], k_rows, n_keys(units[w][1]), mask)
                   for w in range(lookahead)}
        worst = None
        for w, unit in enumerate(units):
            nxt = w + lookahead
            if nxt < len(units):
                pending[nxt] = scores(units[nxt], k_rows, n_keys(units[nxt][1]), mask)
            outside = update(unit, pending.pop(w), vd_cols, vf_cols, n_keys(unit[1]),
                             mode, src, dst, growth)
            if outside is not None:
                worst = outside if worst is None else jnp.maximum(worst, outside)
        return worst

    def key_tile(g):
        row0 = pl.multiple_of(g * tk, tk)
        return (lambda p, n: k_ref[0, p, pl.ds(row0, n), :],
                lambda h, n: vd_ref[0, g, h, :, 0:n],
                lambda h, n: vf_ref[0, g, h, :, 0:n])

    for start_par, start_state in ((0, state_a), (1, state_b)):
        @pl.when(jnp.logical_and(kg == 0, qi % 2 == start_par))
        def _():
            sweep(lambda p, n: km_ref[p], lambda h, n: vdm_ref[h], lambda h, n: vfm_ref[h],
                  lambda cb: N_META, None, "first", None, start_state, lookahead=META_LOOKAHEAD)
            par_scr[0] = start_par

    def copy_b_to_a():
        for dst_ref, src_ref in zip(state_a, state_b):
            for i in range(src_ref.shape[0]):
                dst_ref[i] = src_ref[i]

    first = kg * n_group
    cend_row = pl.program_id(0) * (n_tiles + 1) + first

    def full_tile(g, carry):
        par = par_scr[0]
        prev = (cend_row + g) * FOX_HEADS
        growth = [cend_ref[prev + hh] - cend_ref[prev + FOX_HEADS + hh] for hh in range(FOX_HEADS)]

        @pl.when(par == 0)
        def _():
            outside_scr[...] = sweep(*key_tile(g), lambda cb: tk, None, "fast",
                                     state_a, state_b, growth)

        @pl.when(par == 1)
        def _():
            outside_scr[...] = sweep(*key_tile(g), lambda cb: tk, None, "fast",
                                     state_b, state_a, growth)

        redo = jnp.max(outside_scr[...]) > 0.0

        @pl.when(redo)
        def _():
            @pl.when(par == 1)
            def _():
                copy_b_to_a()

            sweep(*key_tile(g), lambda cb: tk, None, "exact", state_a, state_b)

        par_scr[0] = jnp.where(redo, 1, 1 - par)
        return carry

    lax.fori_loop(0, jnp.clip(qi - first, 0, n_group), full_tile, 0)

    @pl.when(jnp.logical_and(qi >= first, qi < first + n_group))
    def _():
        def mask(cb):
            n = (cb + 1) * QCOLS
            key = lax.broadcasted_iota(jnp.int32, (n, QCOLS), 0)
            qry = lax.broadcasted_iota(jnp.int32, (n, QCOLS), 1)
            return key <= qry + cb * QCOLS

        @pl.when(par_scr[0] == 1)
        def _():
            copy_b_to_a()

        sweep(*key_tile(qi - first), lambda cb: (cb + 1) * QCOLS, mask, "last", state_a, state_a,
              lookahead=DIAG_LOOKAHEAD)
        accd_scr, accf_scr = accd_a, accf_a

        lam = (jnp.exp(jnp.sum(lq1_ref[...] * lk1_ref[...], axis=1, keepdims=True))
               - jnp.exp(jnp.sum(lq2_ref[...] * lk2_ref[...], axis=1, keepdims=True))
               + lambda_init)
        for h in range(DIFF_HEADS):
            a1 = accd_scr[2 * h]
            a2 = accd_scr[2 * h + 1]
            o = (a1[0:DIFF_V_DIM] / a1[DIFF_V_DIM:DIFF_V_DIM + 1]
                 - lam * (a2[0:DIFF_V_DIM] / a2[DIFF_V_DIM:DIFF_V_DIM + 1]))
            ms = jnp.mean(o * o, axis=0, keepdims=True)
            o = o * lax.rsqrt(ms + SUBLN_EPS) * subg_ref[...] * (1.0 - lambda_init)
            o_ref[0, :, h * 128:(h + 1) * 128] = o.T.astype(o_ref.dtype)
        for i in range(FOX_HEADS // 2):
            a0 = accf_scr[2 * i]
            a1 = accf_scr[2 * i + 1]
            o = jnp.concatenate(
                [a0[0:FOX_HEAD_DIM] / a0[FOX_HEAD_DIM:FOX_HEAD_DIM + 1],
                 a1[0:FOX_HEAD_DIM] / a1[FOX_HEAD_DIM:FOX_HEAD_DIM + 1]], axis=0)
            c0 = DIFF_HEADS * DIFF_V_DIM + i * 128
            o_ref[0, :, c0:c0 + 128] = o.T.astype(o_ref.dtype)


def _attn(qt, k, vd, vf, km, vdm, vfm, cend, lq1, lk1, lq2, lk2, subg, lambda_init, tile):
    b, nq = qt.shape[0], qt.shape[1]
    s = nq * tile
    grp = min(KV_GROUP, nq)
    assert qt.shape[4] == tile and tile % QCOLS == 0 and nq % grp == 0
    qi_tab = np.concatenate([np.full(i // grp + 1, i, np.int32) for i in range(nq)])
    kg_tab = np.concatenate([np.arange(i // grp + 1, dtype=np.int32) for i in range(nq)])
    n_steps = int(qi_tab.shape[0])
    d_mix = DIFF_HEADS * DIFF_V_DIM + FOX_HEADS * FOX_HEAD_DIM

    def const(shape):
        nd = len(shape)
        return pl.BlockSpec(shape, lambda *_: (0,) * nd)

    n_units = N_QMAPS * (tile // QCOLS)
    grid_spec = pltpu.PrefetchScalarGridSpec(
        num_scalar_prefetch=3,
        grid=(b, n_steps),
        in_specs=[
            pl.BlockSpec((1, 1, N_QMAPS, 128, tile), lambda i, t, qi, kg, ce: (i, qi[t], 0, 0, 0)),
            pl.BlockSpec((1, N_KBLOCKS, grp * tile, 128), lambda i, t, qi, kg, ce: (i, 0, kg[t], 0)),
            pl.BlockSpec((1, grp, DIFF_HEADS, VD_ROWS, tile),
                         lambda i, t, qi, kg, ce: (i, kg[t], 0, 0, 0)),
            pl.BlockSpec((1, grp, FOX_HEADS, VF_ROWS, tile),
                         lambda i, t, qi, kg, ce: (i, kg[t], 0, 0, 0)),
            const(km.shape), const(vdm.shape), const(vfm.shape),
            const(lq1.shape), const(lk1.shape), const(lq2.shape), const(lk2.shape),
            const(subg.shape),
        ],
        out_specs=pl.BlockSpec((1, tile, d_mix), lambda i, t, qi, kg, ce: (i, qi[t], 0)),
        scratch_shapes=2 * [pltpu.VMEM((n_units, 1, QCOLS), F32),
                            pltpu.VMEM((n_units, 1, QCOLS), F32),
                            pltpu.VMEM((2 * DIFF_HEADS, VD_ROWS, tile), F32),
                            pltpu.VMEM((FOX_HEADS, VF_ROWS, tile), F32)]
        + [pltpu.VMEM((1, QCOLS), F32), pltpu.SMEM((1,), jnp.int32)],
    )
    return pl.pallas_call(
        functools.partial(_attn_kernel, lambda_init=lambda_init, n_tiles=nq),
        grid_spec=grid_spec,
        out_shape=jax.ShapeDtypeStruct((b, s, d_mix), BF16),
        compiler_params=pltpu.CompilerParams(dimension_semantics=("parallel", "arbitrary"),
                                             vmem_limit_bytes=VMEM_LIMIT),
        name="attn",
    )(jnp.asarray(qi_tab), jnp.asarray(kg_tab), cend.reshape(-1), qt, k, vd, vf, km, vdm, vfm,
      lq1, lk1, lq2, lk2, subg)


def _rope_swap_columns():
    n = 2 * DIFF_HEADS * DIFF_QK_DIM
    idx = np.arange(n)
    sgn = np.zeros(n, np.float32)
    r = idx % DIFF_QK_DIM
    half = ROPE_DIMS // 2
    first = r < half
    second = (r >= half) & (r < ROPE_DIMS)
    src = np.where(first, idx + half, np.where(second, idx - half, idx))
    sgn[first] = -1.0
    sgn[second] = 1.0
    return src.astype(np.int32), sgn


def _placement_matrix():
    m = np.zeros((128, FOX_HEADS * 128), np.float32)
    for p in range(3):
        for h in range(FOX_HEADS):
            m[p * FOX_HEADS + h, h * 128 + FOX_HEAD_DIM + p] = 1.0
    return m


def kernel(x, meta_tokens, ffn1_norm_g, ffn1_w_gate_up, ffn1_w_down, mix_norm_g, w_in, b_forget, lam_q1, lam_k1, lam_q2, lam_k2, diff_subln_g, w_out, ffn2_norm_g, ffn2_w_gate_up, ffn2_w_down, final_norm_g):
    b, s, d = x.shape
    assert ffn1_norm_g.shape[0] == 1, "single-layer problem: meta rows are not carried past the mixer"
    assert s % ATTN_TILE == 0 and ROW_TILE == ATTN_TILE
    layer = 0
    lambda_init = _lambda_init(layer)

    row = lambda v: v.reshape(1, -1).astype(F32)
    wgu1 = ffn1_w_gate_up[layer].astype(BF16)
    wd1 = ffn1_w_down[layer].astype(BF16)
    wgu2 = ffn2_w_gate_up[layer].astype(BF16)
    wd2 = ffn2_w_down[layer].astype(BF16)
    wout = w_out[layer].astype(BF16)
    w = w_in[layer]
    n512 = 512
    dq_w, dk_w, dv_w, fq_w, fk_w, fv_w = (w[:, i * n512:(i + 1) * n512] for i in range(6))
    fl_w = w[:, 6 * n512:]
    src, sgn = _rope_swap_columns()
    dk_sw = jnp.take(dk_w, jnp.asarray(src), axis=1) * jnp.asarray(sgn)[None, :]
    fl_blk = jnp.pad(fl_w, ((0, 0), (0, 128 - FOX_HEADS)))
    wrows = jnp.concatenate([dk_w, dk_sw, fk_w, fl_blk], axis=1).astype(BF16)
    q_scale = QK_SCALE * LOG2E
    wt = jnp.concatenate([dq_w * q_scale, fq_w * q_scale, dv_w, fv_w], axis=1).T.astype(BF16)
    bfg = jnp.pad(b_forget[layer].reshape(1, -1).astype(F32), ((0, 0), (0, 128 - FOX_HEADS)))
    place = jnp.asarray(_placement_matrix()).astype(BF16)
    subg = diff_subln_g[layer].reshape(-1, 1).astype(F32)

    n_pos = max(N_META + s, META_TILE)
    pos = jnp.arange(n_pos, dtype=F32)
    inv_freq = jnp.power(ROPE_THETA, -jnp.arange(0, ROPE_DIMS, 2, dtype=F32) / ROPE_DIMS)
    ang = pos[:, None] * inv_freq[None, :]
    cos, sin = jnp.cos(ang), jnp.sin(ang)
    lane_r = np.arange(128) % DIFF_QK_DIM
    in_rope = jnp.asarray(lane_r < ROPE_DIMS)[None, :]
    lane_f = jnp.asarray(lane_r % (ROPE_DIMS // 2))
    cc_all = jnp.where(in_rope, jnp.take(cos, lane_f, axis=1), 1.0)
    ss_all = jnp.where(in_rope, jnp.take(sin, lane_f, axis=1), 0.0)
    cost_all, sint_all = cos.T, sin.T

    hm = jnp.pad(meta_tokens.astype(F32), ((0, META_TILE - N_META), (0, 0)))
    hm1 = _ffn1(hm, row(ffn1_norm_g[layer]), wgu1, wd1, META_TILE)
    zero_carry = jnp.zeros((8, 128), F32)
    _, km, vdm, vfm, carry_m = _proj(
        hm1[None], row(mix_norm_g[layer]), wrows, wt, bfg,
        cc_all[:META_TILE], ss_all[:META_TILE], cost_all[:, :META_TILE], sint_all[:, :META_TILE],
        zero_carry, place, META_TILE, N_META)

    h1 = _ffn1(x.reshape(b * s, d), row(ffn1_norm_g[layer]), wgu1, wd1, FFN_TILE)
    qt, k, vd, vf, cum_end = _proj(
        h1.reshape(b, s, d), row(mix_norm_g[layer]), wrows, wt, bfg,
        cc_all[N_META:N_META + s], ss_all[N_META:N_META + s],
        cost_all[:, N_META:N_META + s], sint_all[:, N_META:N_META + s],
        carry_m[0, 0], place, ROW_TILE, ROW_TILE)
    meta_end = jnp.broadcast_to(carry_m[0, 0, 0, :FOX_HEADS], (b, 1, FOX_HEADS))
    cend = jnp.concatenate([meta_end, cum_end[:, :, 0, :FOX_HEADS]], axis=1) * LOG2E
    o = _attn(qt, k, vd, vf, km[0, :, :N_META, :], vdm[0, 0, :, :, :N_META], vfm[0, 0, :, :, :N_META],
              cend, row(lam_q1[layer]), row(lam_k1[layer]), row(lam_q2[layer]), row(lam_k2[layer]),
              subg, lambda_init, ATTN_TILE)
    out = _tail(o.reshape(b * s, -1), h1, wout, row(ffn2_norm_g[layer]), wgu2, wd2,
                row(final_norm_g), FFN_TILE)
    return out.reshape(b, s, d)
```

```python
import functools
import math

import numpy as np
import jax
import jax.numpy as jnp
from jax import lax
from jax.experimental import pallas as pl
from jax.experimental.pallas import tpu as pltpu

F32 = jnp.float32
BF16 = jnp.bfloat16

N_META = 16
DIFF_HEADS = 4
DIFF_V_DIM = 128
DIFF_QK_DIM = 64
FOX_HEADS = 8
FOX_HEAD_DIM = 64
ROPE_DIMS = 16
ROPE_THETA = 500000.0
RMS_EPS = 1e-6
SUBLN_EPS = 1e-5
NEG_INF = -1e30
QK_SCALE = 0.125
LOG2E = math.log2(math.e)

N_QMAPS = 2 * DIFF_HEADS + FOX_HEADS
N_KBLOCKS = DIFF_HEADS + FOX_HEADS
BF16_SUBLANES = 16
VD_ROWS = DIFF_V_DIM + BF16_SUBLANES
VF_ROWS = FOX_HEAD_DIM + BF16_SUBLANES
META_TILE = 128
FFN_CHUNK = 256
ROW_TILE = 512
FFN_TILE = 1024
ATTN_TILE = 512
QCOLS = 256
FAST_OVER = 20.0
FAST_UNDER = 40.0
KV_GROUP = 4
SCORE_LOOKAHEAD = 3
DIAG_LOOKAHEAD = 5
META_LOOKAHEAD = 12
V7X_VMEM_BYTES = 64 * 1024 * 1024
VMEM_LIMIT = V7X_VMEM_BYTES - 8 * 1024 * 1024


def _lambda_init(layer_idx):
    return 0.8 - 0.6 * math.exp(-0.3 * layer_idx)


def _rms(x, g, eps):
    ms = jnp.mean(x * x, axis=-1, keepdims=True)
    return x * lax.rsqrt(ms + eps) * g


def _split3(x):
    hi = x.astype(BF16)
    r = x - hi.astype(F32)
    mid = r.astype(BF16)
    lo = (r - mid.astype(F32)).astype(BF16)
    return hi, mid, lo


def _const_spec(shape):
    nd = len(shape)
    return pl.BlockSpec(shape, lambda *_: (0,) * nd, pipeline_mode=pl.Buffered(1))


def _swiglu_half_step(h, g_ref, wgu_ref, wd_ref):
    d_ff = wd_ref.shape[0]
    hn = _rms(h, g_ref[...], RMS_EPS).astype(BF16)
    acc = jnp.zeros(h.shape, F32)
    for c in range(d_ff // FFN_CHUNK):
        lo, hi = c * FFN_CHUNK, (c + 1) * FFN_CHUNK
        gate = jnp.dot(hn, wgu_ref[:, lo:hi], preferred_element_type=F32)
        up = jnp.dot(hn, wgu_ref[:, d_ff + lo:d_ff + hi], preferred_element_type=F32)
        act = (gate / (1.0 + jnp.exp(-gate)) * up).astype(BF16)
        acc = acc + jnp.dot(act, wd_ref[lo:hi, :], preferred_element_type=F32)
    return h + 0.5 * acc


def _ffn1_kernel(h_ref, g_ref, wgu_ref, wd_ref, o_ref):
    o_ref[...] = _swiglu_half_step(h_ref[...], g_ref, wgu_ref, wd_ref)


def _ffn1(h, g, wgu, wd, tm):
    rows, d = h.shape
    return pl.pallas_call(
        _ffn1_kernel,
        grid=(rows // tm,),
        in_specs=[pl.BlockSpec((tm, d), lambda i: (i, 0)),
                  _const_spec(g.shape), _const_spec(wgu.shape), _const_spec(wd.shape)],
        out_specs=pl.BlockSpec((tm, d), lambda i: (i, 0)),
        out_shape=jax.ShapeDtypeStruct((rows, d), F32),
        compiler_params=pltpu.CompilerParams(dimension_semantics=("parallel",),
                                             vmem_limit_bytes=VMEM_LIMIT),
        name="ffn1",
    )(h, g, wgu, wd)


def _tail_kernel(o_ref, h_ref, wout_ref, g_ref, wgu_ref, wd_ref, gf_ref, out_ref):
    h2 = h_ref[...] + jnp.dot(o_ref[...], wout_ref[...], preferred_element_type=F32)
    h3 = _swiglu_half_step(h2, g_ref, wgu_ref, wd_ref)
    out_ref[...] = _rms(h3, gf_ref[...], RMS_EPS)


def _tail(o, h1, wout, g, wgu, wd, gf, tm):
    rows, d = h1.shape
    return pl.pallas_call(
        _tail_kernel,
        grid=(rows // tm,),
        in_specs=[pl.BlockSpec((tm, d), lambda i: (i, 0)),
                  pl.BlockSpec((tm, d), lambda i: (i, 0)),
                  _const_spec(wout.shape), _const_spec(g.shape), _const_spec(wgu.shape),
                  _const_spec(wd.shape), _const_spec(gf.shape)],
        out_specs=pl.BlockSpec((tm, d), lambda i: (i, 0)),
        out_shape=jax.ShapeDtypeStruct((rows, d), F32),
        compiler_params=pltpu.CompilerParams(dimension_semantics=("parallel",),
                                             vmem_limit_bytes=VMEM_LIMIT),
        name="tail",
    )(o, h1, wout, g, wgu, wd, gf)


def _proj_kernel(h_ref, g_ref, wrows_ref, wt_ref, bf_ref, cc_ref, ss_ref, cost_ref, sint_ref,
                 carry0_ref, place_ref,
                 qt_ref, k_ref, vd_ref, vf_ref, cend_ref, carry_scr, *, n_valid):
    tm = h_ref.shape[1]

    @pl.when(pl.program_id(1) == 0)
    def _():
        carry_scr[...] = carry0_ref[...]

    hn = _rms(h_ref[0], g_ref[...], RMS_EPS).astype(BF16)
    kr = jnp.dot(hn, wrows_ref[...], preferred_element_type=F32)
    pt = lax.dot_general(wt_ref[...], hn, (((1,), (1,)), ((), ())),
                         preferred_element_type=F32)

    cc = cc_ref[...]
    ss = ss_ref[...]
    n_dk = DIFF_HEADS * 128
    for p in range(DIFF_HEADS):
        kd = kr[:, p * 128:(p + 1) * 128]
        ks = kr[:, n_dk + p * 128:n_dk + (p + 1) * 128]
        k_ref[0, p] = (kd * cc + ks * ss).astype(BF16)

    fk0 = 2 * n_dk
    fl0 = fk0 + FOX_HEADS * FOX_HEAD_DIM
    fl = kr[:, fl0:fl0 + 128] + bf_ref[...]
    logf = jnp.minimum(fl, 0.0) - jnp.log(1.0 + jnp.exp(-jnp.abs(fl)))
    lane = lax.broadcasted_iota(jnp.int32, (tm, 128), 1)
    keep = lane < FOX_HEADS
    if n_valid < tm:
        row = lax.broadcasted_iota(jnp.int32, (tm, 128), 0)
        keep = jnp.logical_and(keep, row < n_valid)
    logf = jnp.where(keep, logf, 0.0)
    parts = jnp.concatenate(_split3(logf), axis=1)
    r_i = lax.broadcasted_iota(jnp.int32, (tm, tm), 0)
    c_i = lax.broadcasted_iota(jnp.int32, (tm, tm), 1)
    tri = jnp.where(r_i >= c_i, 1.0, 0.0).astype(BF16)
    cs = jnp.dot(tri, parts, preferred_element_type=F32)
    cum = cs[:, 0:128] + cs[:, 128:256] + cs[:, 256:384] + carry_scr[0:1, :]
    carry_scr[...] = jnp.broadcast_to(cum[tm - 1:tm, :], carry_scr.shape)
    cend_ref[0, 0] = carry_scr[...]
    chi, cmid, clo = _split3(cum * LOG2E)
    cparts = (chi.astype(F32) + pltpu.roll(cmid.astype(F32), FOX_HEADS, 1)
              + pltpu.roll(clo.astype(F32), 2 * FOX_HEADS, 1)).astype(BF16)
    aug = jnp.dot(cparts, place_ref[...], preferred_element_type=F32)
    low_half = lane < FOX_HEAD_DIM
    for pr in range(FOX_HEADS // 2):
        pair = kr[:, fk0 + pr * 128:fk0 + (pr + 1) * 128]
        for c, blk in enumerate((pair, pltpu.roll(pair, FOX_HEAD_DIM, 1))):
            hh = 2 * pr + c
            kf = jnp.where(low_half, blk, aug[:, hh * 128:(hh + 1) * 128])
            k_ref[0, DIFF_HEADS + hh] = kf.astype(BF16)

    cost = cost_ref[...]
    sint = sint_ref[...]
    half = ROPE_DIMS // 2
    zeros64 = jnp.zeros((64, tm), BF16)
    for h in range(DIFF_HEADS):
        for c in range(2):
            base = h * 128 + c * 64
            x1 = pt[base:base + half]
            x2 = pt[base + half:base + ROPE_DIMS]
            blk = jnp.concatenate([x1 * cost - x2 * sint, x2 * cost + x1 * sint,
                                   pt[base + ROPE_DIMS:base + 64]], axis=0).astype(BF16)
            u = 2 * h + c
            qt_ref[0, 0, u, c * 64:(c + 1) * 64, :] = blk
            qt_ref[0, 0, u, (1 - c) * 64:(2 - c) * 64, :] = zeros64
    fq0 = DIFF_HEADS * 128
    rows64 = lax.broadcasted_iota(jnp.int32, (64, tm), 0)
    neg_rows = jnp.where(rows64 < 3, -1.0, 0.0).astype(BF16)
    for hh in range(FOX_HEADS):
        u = 2 * DIFF_HEADS + hh
        qt_ref[0, 0, u, 0:64, :] = pt[fq0 + hh * 64:fq0 + (hh + 1) * 64].astype(BF16)
        qt_ref[0, 0, u, 64:128, :] = neg_rows

    ones_rows = jnp.ones((BF16_SUBLANES, tm), BF16)
    dv0 = fq0 + FOX_HEADS * 64
    for h in range(DIFF_HEADS):
        vd_ref[0, 0, h, 0:DIFF_V_DIM, :] = pt[dv0 + h * 128:dv0 + (h + 1) * 128].astype(BF16)
        vd_ref[0, 0, h, DIFF_V_DIM:VD_ROWS, :] = ones_rows
    fv0 = dv0 + DIFF_HEADS * 128
    for hh in range(FOX_HEADS):
        vf_ref[0, 0, hh, 0:FOX_HEAD_DIM, :] = pt[fv0 + hh * 64:fv0 + (hh + 1) * 64].astype(BF16)
        vf_ref[0, 0, hh, FOX_HEAD_DIM:VF_ROWS, :] = ones_rows


def _proj(h, g, wrows, wt, bfg, cc, ss, cost, sint, carry0, place, tm, n_valid):
    b, s, d = h.shape
    nt = s // tm
    out_shape = (
        jax.ShapeDtypeStruct((b, nt, N_QMAPS, 128, tm), BF16),
        jax.ShapeDtypeStruct((b, N_KBLOCKS, s, 128), BF16),
        jax.ShapeDtypeStruct((b, nt, DIFF_HEADS, VD_ROWS, tm), BF16),
        jax.ShapeDtypeStruct((b, nt, FOX_HEADS, VF_ROWS, tm), BF16),
        jax.ShapeDtypeStruct((b, nt, 8, 128), F32),
    )
    return pl.pallas_call(
        functools.partial(_proj_kernel, n_valid=n_valid),
        grid=(b, nt),
        in_specs=[pl.BlockSpec((1, tm, d), lambda i, t: (i, t, 0)),
                  _const_spec(g.shape), _const_spec(wrows.shape), _const_spec(wt.shape),
                  _const_spec(bfg.shape),
                  pl.BlockSpec((tm, 128), lambda i, t: (t, 0)),
                  pl.BlockSpec((tm, 128), lambda i, t: (t, 0)),
                  pl.BlockSpec((8, tm), lambda i, t: (0, t)),
                  pl.BlockSpec((8, tm), lambda i, t: (0, t)),
                  _const_spec(carry0.shape), _const_spec(place.shape)],
        out_specs=(pl.BlockSpec((1, 1, N_QMAPS, 128, tm), lambda i, t: (i, t, 0, 0, 0)),
                   pl.BlockSpec((1, N_KBLOCKS, tm, 128), lambda i, t: (i, 0, t, 0)),
                   pl.BlockSpec((1, 1, DIFF_HEADS, VD_ROWS, tm), lambda i, t: (i, t, 0, 0, 0)),
                   pl.BlockSpec((1, 1, FOX_HEADS, VF_ROWS, tm), lambda i, t: (i, t, 0, 0, 0)),
                   pl.BlockSpec((1, 1, 8, 128), lambda i, t: (i, t, 0, 0))),
        out_shape=out_shape,
        scratch_shapes=[pltpu.VMEM((8, 128), F32)],
        compiler_params=pltpu.CompilerParams(dimension_semantics=("parallel", "arbitrary"),
                                             vmem_limit_bytes=VMEM_LIMIT),
        name="proj",
    )(h, g, wrows, wt, bfg, cc, ss, cost, sint, carry0, place)


def _attn_kernel(qi_ref, kg_ref, cend_ref,
                 qt_ref, k_ref, vd_ref, vf_ref, km_ref, vdm_ref, vfm_ref,
                 lq1_ref, lk1_ref, lq2_ref, lk2_ref, subg_ref,
                 o_ref, ref_a, max_a, accd_a, accf_a, ref_b, max_b, accd_b, accf_b,
                 outside_scr, par_scr, *, lambda_init, n_tiles):
    state_a = (ref_a, max_a, accd_a, accf_a)
    state_b = (ref_b, max_b, accd_b, accf_b)
    s_idx = pl.program_id(1)
    qi = qi_ref[s_idx]
    kg = kg_ref[s_idx]
    tq = qt_ref.shape[4]
    n_group = vd_ref.shape[1]
    tk = k_ref.shape[2] // n_group
    assert tq % QCOLS == 0 and tq == tk

    n_dmaps = 2 * DIFF_HEADS
    n_cb = tq // QCOLS
    units = [(u, cb) for u in range(N_QMAPS) for cb in range(n_cb)]

    def scores(unit, k_rows, n, mask):
        u, cb = unit
        kb = u // 2 if u < n_dmaps else u - DIFF_HEADS
        st = jnp.dot(k_rows(kb, n), qt_ref[0, 0, u, :, cb * QCOLS:(cb + 1) * QCOLS],
                     preferred_element_type=F32)
        if mask is not None:
            st = jnp.where(mask(cb), st, NEG_INF)
        return st, jnp.max(st, axis=0, keepdims=True)

    def update(unit, st_max, vd_cols, vf_cols, n, mode, src, dst, growth):
        u, cb = unit
        st, t_max = st_max
        idx = u * n_cb + cb
        cols = slice(cb * QCOLS, (cb + 1) * QCOLS)
        ref_dst, max_dst = dst[0], dst[1]
        if u < n_dmaps:
            group, a, v_t = 2, u, vd_cols(u // 2, n)
        else:
            group, a, v_t = 3, u - n_dmaps, vf_cols(u - n_dmaps, n)
        outside = None
        if mode == "first":
            shift = t_max
        else:
            r_old = src[0][idx]
            if mode == "fast":
                x_old = src[1][idx]
                shift = x_old + growth[u - n_dmaps] if u >= n_dmaps else x_old
            else:
                shift = jnp.maximum(r_old, t_max)
        p = jnp.exp2(st - shift).astype(BF16)
        pv = jnp.dot(v_t, p, preferred_element_type=F32)
        if mode == "first":
            dst[group][a, :, cols] = pv
            max_dst[idx] = t_max
        else:
            dst[group][a, :, cols] = src[group][a, :, cols] * jnp.exp2(r_old - shift) + pv
            if mode == "fast":
                x_new = jnp.maximum(x_old, t_max)
                max_dst[idx] = x_new
                outside = jnp.maximum(t_max - shift - FAST_OVER, shift - x_new - FAST_UNDER)
            elif mode == "exact":
                max_dst[idx] = jnp.maximum(src[1][idx], t_max)
        ref_dst[idx] = shift
        return outside

    def sweep(k_rows, vd_cols, vf_cols, n_keys, mask, mode, src, dst, growth=None,
              lookahead=SCORE_LOOKAHEAD):
        pending = {w: scores(units[w], k_rows, n_keys(units[w][1]), mask)
                   for w in range(lookahead)}
        worst = None
        for w, unit in enumerate(units):
            nxt = w + lookahead
            if nxt < len(units):
                pending[nxt] = scores(units[nxt], k_rows, n_keys(units[nxt][1]), mask)
            outside = update(unit, pending.pop(w), vd_cols, vf_cols, n_keys(unit[1]),
                             mode, src, dst, growth)
            if outside is not None:
                worst = outside if worst is None else jnp.maximum(worst, outside)
        return worst

    def key_tile(g):
        row0 = pl.multiple_of(g * tk, tk)
        return (lambda p, n: k_ref[0, p, pl.ds(row0, n), :],
                lambda h, n: vd_ref[0, g, h, :, 0:n],
                lambda h, n: vf_ref[0, g, h, :, 0:n])

    for start_par, start_state in ((0, state_a), (1, state_b)):
        @pl.when(jnp.logical_and(kg == 0, qi % 2 == start_par))
        def _():
            sweep(lambda p, n: km_ref[p], lambda h, n: vdm_ref[h], lambda h, n: vfm_ref[h],
                  lambda cb: N_META, None, "first", None, start_state, lookahead=META_LOOKAHEAD)
            par_scr[0] = start_par

    def copy_b_to_a():
        for dst_ref, src_ref in zip(state_a, state_b):
            for i in range(src_ref.shape[0]):
                dst_ref[i] = src_ref[i]

    first = kg * n_group
    cend_row = pl.program_id(0) * (n_tiles + 1) + first

    def full_tile(g, carry):
        par = par_scr[0]
        prev = (cend_row + g) * FOX_HEADS
        growth = [cend_ref[prev + hh] - cend_ref[prev + FOX_HEADS + hh] for hh in range(FOX_HEADS)]

        @pl.when(par == 0)
        def _():
            outside_scr[...] = sweep(*key_tile(g), lambda cb: tk, None, "fast",
                                     state_a, state_b, growth)

        @pl.when(par == 1)
        def _():
            outside_scr[...] = sweep(*key_tile(g), lambda cb: tk, None, "fast",
                                     state_b, state_a, growth)

        redo = jnp.max(outside_scr[...]) > 0.0

        @pl.when(redo)
        def _():
            @pl.when(par == 1)
            def _():
                copy_b_to_a()

            sweep(*key_tile(g), lambda cb: tk, None, "exact", state_a, state_b)

        par_scr[0] = jnp.where(redo, 1, 1 - par)
        return carry

    lax.fori_loop(0, jnp.clip(qi - first, 0, n_group), full_tile, 0)

    @pl.when(jnp.logical_and(qi >= first, qi < first + n_group))
    def _():
        def mask(cb):
            n = (cb + 1) * QCOLS
            key = lax.broadcasted_iota(jnp.int32, (n, QCOLS), 0)
            qry = lax.broadcasted_iota(jnp.int32, (n, QCOLS), 1)
            return key <= qry + cb * QCOLS

        @pl.when(par_scr[0] == 1)
        def _():
            copy_b_to_a()

        sweep(*key_tile(qi - first), lambda cb: (cb + 1) * QCOLS, mask, "last", state_a, state_a,
              lookahead=DIAG_LOOKAHEAD)
        accd_scr, accf_scr = accd_a, accf_a

        lam = (jnp.exp(jnp.sum(lq1_ref[...] * lk1_ref[...], axis=1, keepdims=True))
               - jnp.exp(jnp.sum(lq2_ref[...] * lk2_ref[...], axis=1, keepdims=True))
               + lambda_init)
        for h in range(DIFF_HEADS):
            a1 = accd_scr[2 * h]
            a2 = accd_scr[2 * h + 1]
            o = (a1[0:DIFF_V_DIM] / a1[DIFF_V_DIM:DIFF_V_DIM + 1]
                 - lam * (a2[0:DIFF_V_DIM] / a2[DIFF_V_DIM:DIFF_V_DIM + 1]))
            ms = jnp.mean(o * o, axis=0, keepdims=True)
            o = o * lax.rsqrt(ms + SUBLN_EPS) * subg_ref[...] * (1.0 - lambda_init)
            o_ref[0, :, h * 128:(h + 1) * 128] = o.T.astype(o_ref.dtype)
        for i in range(FOX_HEADS // 2):
            a0 = accf_scr[2 * i]
            a1 = accf_scr[2 * i + 1]
            o = jnp.concatenate(
                [a0[0:FOX_HEAD_DIM] / a0[FOX_HEAD_DIM:FOX_HEAD_DIM + 1],
                 a1[0:FOX_HEAD_DIM] / a1[FOX_HEAD_DIM:FOX_HEAD_DIM + 1]], axis=0)
            c0 = DIFF_HEADS * DIFF_V_DIM + i * 128
            o_ref[0, :, c0:c0 + 128] = o.T.astype(o_ref.dtype)


def _attn(qt, k, vd, vf, km, vdm, vfm, cend, lq1, lk1, lq2, lk2, subg, lambda_init, tile):
    b, nq = qt.shape[0], qt.shape[1]
    s = nq * tile
    grp = min(KV_GROUP, nq)
    assert qt.shape[4] == tile and tile % QCOLS == 0 and nq % grp == 0
    qi_tab = np.concatenate([np.full(i // grp + 1, i, np.int32) for i in range(nq)])
    kg_tab = np.concatenate([np.arange(i // grp + 1, dtype=np.int32) for i in range(nq)])
    n_steps = int(qi_tab.shape[0])
    d_mix = DIFF_HEADS * DIFF_V_DIM + FOX_HEADS * FOX_HEAD_DIM

    def const(shape):
        nd = len(shape)
        return pl.BlockSpec(shape, lambda *_: (0,) * nd)

    n_units = N_QMAPS * (tile // QCOLS)
    grid_spec = pltpu.PrefetchScalarGridSpec(
        num_scalar_prefetch=3,
        grid=(b, n_steps),
        in_specs=[
            pl.BlockSpec((1, 1, N_QMAPS, 128, tile), lambda i, t, qi, kg, ce: (i, qi[t], 0, 0, 0)),
            pl.BlockSpec((1, N_KBLOCKS, grp * tile, 128), lambda i, t, qi, kg, ce: (i, 0, kg[t], 0)),
            pl.BlockSpec((1, grp, DIFF_HEADS, VD_ROWS, tile),
                         lambda i, t, qi, kg, ce: (i, kg[t], 0, 0, 0)),
            pl.BlockSpec((1, grp, FOX_HEADS, VF_ROWS, tile),
                         lambda i, t, qi, kg, ce: (i, kg[t], 0, 0, 0)),
            const(km.shape), const(vdm.shape), const(vfm.shape),
            const(lq1.shape), const(lk1.shape), const(lq2.shape), const(lk2.shape),
            const(subg.shape),
        ],
        out_specs=pl.BlockSpec((1, tile, d_mix), lambda i, t, qi, kg, ce: (i, qi[t], 0)),
        scratch_shapes=2 * [pltpu.VMEM((n_units, 1, QCOLS), F32),
                            pltpu.VMEM((n_units, 1, QCOLS), F32),
                            pltpu.VMEM((2 * DIFF_HEADS, VD_ROWS, tile), F32),
                            pltpu.VMEM((FOX_HEADS, VF_ROWS, tile), F32)]
        + [pltpu.VMEM((1, QCOLS), F32), pltpu.SMEM((1,), jnp.int32)],
    )
    return pl.pallas_call(
        functools.partial(_attn_kernel, lambda_init=lambda_init, n_tiles=nq),
        grid_spec=grid_spec,
        out_shape=jax.ShapeDtypeStruct((b, s, d_mix), BF16),
        compiler_params=pltpu.CompilerParams(dimension_semantics=("parallel", "arbitrary"),
                                             vmem_limit_bytes=VMEM_LIMIT),
        name="attn",
    )(jnp.asarray(qi_tab), jnp.asarray(kg_tab), cend.reshape(-1), qt, k, vd, vf, km, vdm, vfm,
      lq1, lk1, lq2, lk2, subg)


def _rope_swap_columns():
    n = 2 * DIFF_HEADS * DIFF_QK_DIM
    idx = np.arange(n)
    sgn = np.zeros(n, np.float32)
    r = idx % DIFF_QK_DIM
    half = ROPE_DIMS // 2
    first = r < half
    second = (r >= half) & (r < ROPE_DIMS)
    src = np.where(first, idx + half, np.where(second, idx - half, idx))
    sgn[first] = -1.0
    sgn[second] = 1.0
    return src.astype(np.int32), sgn


def _placement_matrix():
    m = np.zeros((128, FOX_HEADS * 128), np.float32)
    for p in range(3):
        for h in range(FOX_HEADS):
            m[p * FOX_HEADS + h, h * 128 + FOX_HEAD_DIM + p] = 1.0
    return m


def kernel(x, meta_tokens, ffn1_norm_g, ffn1_w_gate_up, ffn1_w_down, mix_norm_g, w_in, b_forget, lam_q1, lam_k1, lam_q2, lam_k2, diff_subln_g, w_out, ffn2_norm_g, ffn2_w_gate_up, ffn2_w_down, final_norm_g):
    b, s, d = x.shape
    assert ffn1_norm_g.shape[0] == 1, "single-layer problem: meta rows are not carried past the mixer"
    assert s % ATTN_TILE == 0 and ROW_TILE == ATTN_TILE
    layer = 0
    lambda_init = _lambda_init(layer)

    row = lambda v: v.reshape(1, -1).astype(F32)
    wgu1 = ffn1_w_gate_up[layer].astype(BF16)
    wd1 = ffn1_w_down[layer].astype(BF16)
    wgu2 = ffn2_w_gate_up[layer].astype(BF16)
    wd2 = ffn2_w_down[layer].astype(BF16)
    wout = w_out[layer].astype(BF16)
    w = w_in[layer]
    n512 = 512
    dq_w, dk_w, dv_w, fq_w, fk_w, fv_w = (w[:, i * n512:(i + 1) * n512] for i in range(6))
    fl_w = w[:, 6 * n512:]
    src, sgn = _rope_swap_columns()
    dk_sw = jnp.take(dk_w, jnp.asarray(src), axis=1) * jnp.asarray(sgn)[None, :]
    fl_blk = jnp.pad(fl_w, ((0, 0), (0, 128 - FOX_HEADS)))
    wrows = jnp.concatenate([dk_w, dk_sw, fk_w, fl_blk], axis=1).astype(BF16)
    q_scale = QK_SCALE * LOG2E
    wt = jnp.concatenate([dq_w * q_scale, fq_w * q_scale, dv_w, fv_w], axis=1).T.astype(BF16)
    bfg = jnp.pad(b_forget[layer].reshape(1, -1).astype(F32), ((0, 0), (0, 128 - FOX_HEADS)))
    place = jnp.asarray(_placement_matrix()).astype(BF16)
    subg = diff_subln_g[layer].reshape(-1, 1).astype(F32)

    n_pos = max(N_META + s, META_TILE)
    pos = jnp.arange(n_pos, dtype=F32)
    inv_freq = jnp.power(ROPE_THETA, -jnp.arange(0, ROPE_DIMS, 2, dtype=F32) / ROPE_DIMS)
    ang = pos[:, None] * inv_freq[None, :]
    cos, sin = jnp.cos(ang), jnp.sin(ang)
    lane_r = np.arange(128) % DIFF_QK_DIM
    in_rope = jnp.asarray(lane_r < ROPE_DIMS)[None, :]
    lane_f = jnp.asarray(lane_r % (ROPE_DIMS // 2))
    cc_all = jnp.where(in_rope, jnp.take(cos, lane_f, axis=1), 1.0)
    ss_all = jnp.where(in_rope, jnp.take(sin, lane_f, axis=1), 0.0)
    cost_all, sint_all = cos.T, sin.T

    hm = jnp.pad(meta_tokens.astype(F32), ((0, META_TILE - N_META), (0, 0)))
    hm1 = _ffn1(hm, row(ffn1_norm_g[layer]), wgu1, wd1, META_TILE)
    zero_carry = jnp.zeros((8, 128), F32)
    _, km, vdm, vfm, carry_m = _proj(
        hm1[None], row(mix_norm_g[layer]), wrows, wt, bfg,
        cc_all[:META_TILE], ss_all[:META_TILE], cost_all[:, :META_TILE], sint_all[:, :META_TILE],
        zero_carry, place, META_TILE, N_META)

    h1 = _ffn1(x.reshape(b * s, d), row(ffn1_norm_g[layer]), wgu1, wd1, FFN_TILE)
    qt, k, vd, vf, cum_end = _proj(
        h1.reshape(b, s, d), row(mix_norm_g[layer]), wrows, wt, bfg,
        cc_all[N_META:N_META + s], ss_all[N_META:N_META + s],
        cost_all[:, N_META:N_META + s], sint_all[:, N_META:N_META + s],
        carry_m[0, 0], place, ROW_TILE, ROW_TILE)
    meta_end = jnp.broadcast_to(carry_m[0, 0, 0, :FOX_HEADS], (b, 1, FOX_HEADS))
    cend = jnp.concatenate([meta_end, cum_end[:, :, 0, :FOX_HEADS]], axis=1) * LOG2E
    o = _attn(qt, k, vd, vf, km[0, :, :N_META, :], vdm[0, 0, :, :, :N_META], vfm[0, 0, :, :, :N_META],
              cend, row(lam_q1[layer]), row(lam_k1[layer]), row(lam_q2[layer]), row(lam_k2[layer]),
              subg, lambda_init, ATTN_TILE)
    out = _tail(o.reshape(b * s, -1), h1, wout, row(ffn2_norm_g[layer]), wgu2, wd2,
                row(final_norm_g), FFN_TILE)
    return out.reshape(b, s, d)
```

```python
import functools
import math

import numpy as np
import jax
import jax.numpy as jnp
from jax import lax
from jax.experimental import pallas as pl
from jax.experimental.pallas import tpu as pltpu

F32 = jnp.float32
BF16 = jnp.bfloat16

N_META = 16
DIFF_HEADS = 4
DIFF_V_DIM = 128
DIFF_QK_DIM = 64
FOX_HEADS = 8
FOX_HEAD_DIM = 64
ROPE_DIMS = 16
ROPE_THETA = 500000.0
RMS_EPS = 1e-6
SUBLN_EPS = 1e-5
NEG_INF = -1e30
QK_SCALE = 0.125
LOG2E = math.log2(math.e)

N_QMAPS = 2 * DIFF_HEADS + FOX_HEADS
N_KBLOCKS = DIFF_HEADS + FOX_HEADS
BF16_SUBLANES = 16
VD_ROWS = DIFF_V_DIM + BF16_SUBLANES
VF_ROWS = FOX_HEAD_DIM + BF16_SUBLANES
META_TILE = 128
FFN_CHUNK = 256
ROW_TILE = 512
FFN_TILE = 1024
ATTN_TILE = 512
QCOLS = 256
FAST_OVER = 20.0
FAST_UNDER = 40.0
KV_GROUP = 4
SCORE_LOOKAHEAD = 3
DIAG_LOOKAHEAD = 5
META_LOOKAHEAD = 12
V7X_VMEM_BYTES = 64 * 1024 * 1024
VMEM_LIMIT = V7X_VMEM_BYTES - 8 * 1024 * 1024


def _lambda_init(layer_idx):
    return 0.8 - 0.6 * math.exp(-0.3 * layer_idx)


def _rms(x, g, eps):
    ms = jnp.mean(x * x, axis=-1, keepdims=True)
    return x * lax.rsqrt(ms + eps) * g


def _split3(x):
    hi = x.astype(BF16)
    r = x - hi.astype(F32)
    mid = r.astype(BF16)
    lo = (r - mid.astype(F32)).astype(BF16)
    return hi, mid, lo


def _const_spec(shape):
    nd = len(shape)
    return pl.BlockSpec(shape, lambda *_: (0,) * nd, pipeline_mode=pl.Buffered(1))


def _swiglu_half_step(h, g_ref, wgu_ref, wd_ref):
    d_ff = wd_ref.shape[0]
    hn = _rms(h, g_ref[...], RMS_EPS).astype(BF16)
    acc = jnp.zeros(h.shape, F32)
    for c in range(d_ff // FFN_CHUNK):
        lo, hi = c * FFN_CHUNK, (c + 1) * FFN_CHUNK
        gate = jnp.dot(hn, wgu_ref[:, lo:hi], preferred_element_type=F32)
        up = jnp.dot(hn, wgu_ref[:, d_ff + lo:d_ff + hi], preferred_element_type=F32)
        act = (gate / (1.0 + jnp.exp(-gate)) * up).astype(BF16)
        acc = acc + jnp.dot(act, wd_ref[lo:hi, :], preferred_element_type=F32)
    return h + 0.5 * acc


def _ffn1_kernel(h_ref, g_ref, wgu_ref, wd_ref, o_ref):
    o_ref[...] = _swiglu_half_step(h_ref[...], g_ref, wgu_ref, wd_ref)


def _ffn1(h, g, wgu, wd, tm):
    rows, d = h.shape
    return pl.pallas_call(
        _ffn1_kernel,
        grid=(rows // tm,),
        in_specs=[pl.BlockSpec((tm, d), lambda i: (i, 0)),
                  _const_spec(g.shape), _const_spec(wgu.shape), _const_spec(wd.shape)],
        out_specs=pl.BlockSpec((tm, d), lambda i: (i, 0)),
        out_shape=jax.ShapeDtypeStruct((rows, d), F32),
        compiler_params=pltpu.CompilerParams(dimension_semantics=("parallel",),
                                             vmem_limit_bytes=VMEM_LIMIT),
        name="ffn1",
    )(h, g, wgu, wd)


def _tail_kernel(o_ref, h_ref, wout_ref, g_ref, wgu_ref, wd_ref, gf_ref, out_ref):
    h2 = h_ref[...] + jnp.dot(o_ref[...], wout_ref[...], preferred_element_type=F32)
    h3 = _swiglu_half_step(h2, g_ref, wgu_ref, wd_ref)
    out_ref[...] = _rms(h3, gf_ref[...], RMS_EPS)


def _tail(o, h1, wout, g, wgu, wd, gf, tm):
    rows, d = h1.shape
    return pl.pallas_call(
        _tail_kernel,
        grid=(rows // tm,),
        in_specs=[pl.BlockSpec((tm, d), lambda i: (i, 0)),
                  pl.BlockSpec((tm, d), lambda i: (i, 0)),
                  _const_spec(wout.shape), _const_spec(g.shape), _const_spec(wgu.shape),
                  _const_spec(wd.shape), _const_spec(gf.shape)],
        out_specs=pl.BlockSpec((tm, d), lambda i: (i, 0)),
        out_shape=jax.ShapeDtypeStruct((rows, d), F32),
        compiler_params=pltpu.CompilerParams(dimension_semantics=("parallel",),
                                             vmem_limit_bytes=VMEM_LIMIT),
        name="tail",
    )(o, h1, wout, g, wgu, wd, gf)


def _proj_kernel(h_ref, g_ref, wrows_ref, wt_ref, bf_ref, cc_ref, ss_ref, cost_ref, sint_ref,
                 carry0_ref, place_ref,
                 qt_ref, k_ref, vd_ref, vf_ref, cend_ref, carry_scr, *, n_valid):
    tm = h_ref.shape[1]

    @pl.when(pl.program_id(1) == 0)
    def _():
        carry_scr[...] = carry0_ref[...]

    hn = _rms(h_ref[0], g_ref[...], RMS_EPS).astype(BF16)
    kr = jnp.dot(hn, wrows_ref[...], preferred_element_type=F32)
    pt = lax.dot_general(wt_ref[...], hn, (((1,), (1,)), ((), ())),
                         preferred_element_type=F32)

    lane = lax.broadcasted_iota(jnp.int32, (tm, 128), 1)
    half = ROPE_DIMS // 2
    first_group = (lane % DIFF_QK_DIM) < half
    cc = cc_ref[...]
    ss_from_above = jnp.where(first_group, ss_ref[...], 0.0)
    ss_from_below = jnp.where(first_group, 0.0, ss_ref[...])
    n_dk = DIFF_HEADS * 128
    for p in range(DIFF_HEADS):
        kd = kr[:, p * 128:(p + 1) * 128]
        k_ref[0, p] = (kd * cc + pltpu.roll(kd, 128 - half, 1) * ss_from_above
                       + pltpu.roll(kd, half, 1) * ss_from_below).astype(BF16)

    fk0 = n_dk
    fl0 = fk0 + FOX_HEADS * FOX_HEAD_DIM
    fl = kr[:, fl0:fl0 + 128] + bf_ref[...]
    logf = jnp.minimum(fl, 0.0) - jnp.log(1.0 + jnp.exp(-jnp.abs(fl)))
    keep = lane < FOX_HEADS
    if n_valid < tm:
        row = lax.broadcasted_iota(jnp.int32, (tm, 128), 0)
        keep = jnp.logical_and(keep, row < n_valid)
    logf = jnp.where(keep, logf, 0.0)
    parts = jnp.concatenate(_split3(logf), axis=1)
    r_i = lax.broadcasted_iota(jnp.int32, (tm, tm), 0)
    c_i = lax.broadcasted_iota(jnp.int32, (tm, tm), 1)
    tri = jnp.where(r_i >= c_i, 1.0, 0.0).astype(BF16)
    cs = jnp.dot(tri, parts, preferred_element_type=F32)
    cum = cs[:, 0:128] + cs[:, 128:256] + cs[:, 256:384] + carry_scr[0:1, :]
    carry_scr[...] = jnp.broadcast_to(cum[tm - 1:tm, :], carry_scr.shape)
    cend_ref[0, 0] = carry_scr[...]
    chi, cmid, clo = _split3(cum * LOG2E)
    cparts = (chi.astype(F32) + pltpu.roll(cmid.astype(F32), FOX_HEADS, 1)
              + pltpu.roll(clo.astype(F32), 2 * FOX_HEADS, 1)).astype(BF16)
    aug = jnp.dot(cparts, place_ref[...], preferred_element_type=F32)
    low_half = lane < FOX_HEAD_DIM
    for pr in range(FOX_HEADS // 2):
        pair = kr[:, fk0 + pr * 128:fk0 + (pr + 1) * 128]
        for c, blk in enumerate((pair, pltpu.roll(pair, FOX_HEAD_DIM, 1))):
            hh = 2 * pr + c
            kf = jnp.where(low_half, blk, aug[:, hh * 128:(hh + 1) * 128])
            k_ref[0, DIFF_HEADS + hh] = kf.astype(BF16)

    cost = cost_ref[...]
    sint = sint_ref[...]
    half = ROPE_DIMS // 2
    zeros64 = jnp.zeros((64, tm), BF16)
    for h in range(DIFF_HEADS):
        for c in range(2):
            base = h * 128 + c * 64
            x1 = pt[base:base + half]
            x2 = pt[base + half:base + ROPE_DIMS]
            blk = jnp.concatenate([x1 * cost - x2 * sint, x2 * cost + x1 * sint,
                                   pt[base + ROPE_DIMS:base + 64]], axis=0).astype(BF16)
            u = 2 * h + c
            qt_ref[0, 0, u, c * 64:(c + 1) * 64, :] = blk
            qt_ref[0, 0, u, (1 - c) * 64:(2 - c) * 64, :] = zeros64
    fq0 = DIFF_HEADS * 128
    rows64 = lax.broadcasted_iota(jnp.int32, (64, tm), 0)
    neg_rows = jnp.where(rows64 < 3, -1.0, 0.0).astype(BF16)
    for hh in range(FOX_HEADS):
        u = 2 * DIFF_HEADS + hh
        qt_ref[0, 0, u, 0:64, :] = pt[fq0 + hh * 64:fq0 + (hh + 1) * 64].astype(BF16)
        qt_ref[0, 0, u, 64:128, :] = neg_rows

    ones_rows = jnp.ones((BF16_SUBLANES, tm), BF16)
    dv0 = fq0 + FOX_HEADS * 64
    for h in range(DIFF_HEADS):
        vd_ref[0, 0, h, 0:DIFF_V_DIM, :] = pt[dv0 + h * 128:dv0 + (h + 1) * 128].astype(BF16)
        vd_ref[0, 0, h, DIFF_V_DIM:VD_ROWS, :] = ones_rows
    fv0 = dv0 + DIFF_HEADS * 128
    for hh in range(FOX_HEADS):
        vf_ref[0, 0, hh, 0:FOX_HEAD_DIM, :] = pt[fv0 + hh * 64:fv0 + (hh + 1) * 64].astype(BF16)
        vf_ref[0, 0, hh, FOX_HEAD_DIM:VF_ROWS, :] = ones_rows


def _proj(h, g, wrows, wt, bfg, cc, ss, cost, sint, carry0, place, tm, n_valid):
    b, s, d = h.shape
    nt = s // tm
    out_shape = (
        jax.ShapeDtypeStruct((b, nt, N_QMAPS, 128, tm), BF16),
        jax.ShapeDtypeStruct((b, N_KBLOCKS, s, 128), BF16),
        jax.ShapeDtypeStruct((b, nt, DIFF_HEADS, VD_ROWS, tm), BF16),
        jax.ShapeDtypeStruct((b, nt, FOX_HEADS, VF_ROWS, tm), BF16),
        jax.ShapeDtypeStruct((b, nt, 8, 128), F32),
    )
    return pl.pallas_call(
        functools.partial(_proj_kernel, n_valid=n_valid),
        grid=(b, nt),
        in_specs=[pl.BlockSpec((1, tm, d), lambda i, t: (i, t, 0)),
                  _const_spec(g.shape), _const_spec(wrows.shape), _const_spec(wt.shape),
                  _const_spec(bfg.shape),
                  pl.BlockSpec((tm, 128), lambda i, t: (t, 0)),
                  pl.BlockSpec((tm, 128), lambda i, t: (t, 0)),
                  pl.BlockSpec((8, tm), lambda i, t: (0, t)),
                  pl.BlockSpec((8, tm), lambda i, t: (0, t)),
                  _const_spec(carry0.shape), _const_spec(place.shape)],
        out_specs=(pl.BlockSpec((1, 1, N_QMAPS, 128, tm), lambda i, t: (i, t, 0, 0, 0)),
                   pl.BlockSpec((1, N_KBLOCKS, tm, 128), lambda i, t: (i, 0, t, 0)),
                   pl.BlockSpec((1, 1, DIFF_HEADS, VD_ROWS, tm), lambda i, t: (i, t, 0, 0, 0)),
                   pl.BlockSpec((1, 1, FOX_HEADS, VF_ROWS, tm), lambda i, t: (i, t, 0, 0, 0)),
                   pl.BlockSpec((1, 1, 8, 128), lambda i, t: (i, t, 0, 0))),
        out_shape=out_shape,
        scratch_shapes=[pltpu.VMEM((8, 128), F32)],
        compiler_params=pltpu.CompilerParams(dimension_semantics=("parallel", "arbitrary"),
                                             vmem_limit_bytes=VMEM_LIMIT),
        name="proj",
    )(h, g, wrows, wt, bfg, cc, ss, cost, sint, carry0, place)


def _attn_kernel(qi_ref, kg_ref, cend_ref,
                 qt_ref, k_ref, vd_ref, vf_ref, km_ref, vdm_ref, vfm_ref,
                 lq1_ref, lk1_ref, lq2_ref, lk2_ref, subg_ref,
                 o_ref, ref_a, max_a, accd_a, accf_a, ref_b, max_b, accd_b, accf_b,
                 outside_scr, par_scr, *, lambda_init, n_tiles):
    state_a = (ref_a, max_a, accd_a, accf_a)
    state_b = (ref_b, max_b, accd_b, accf_b)
    s_idx = pl.program_id(1)
    qi = qi_ref[s_idx]
    kg = kg_ref[s_idx]
    tq = qt_ref.shape[4]
    n_group = vd_ref.shape[1]
    tk = k_ref.shape[2] // n_group
    assert tq % QCOLS == 0 and tq == tk

    n_dmaps = 2 * DIFF_HEADS
    n_cb = tq // QCOLS
    units = [(u, cb) for u in range(N_QMAPS) for cb in range(n_cb)]

    def scores(unit, k_rows, n, mask):
        u, cb = unit
        kb = u // 2 if u < n_dmaps else u - DIFF_HEADS
        st = jnp.dot(k_rows(kb, n), qt_ref[0, 0, u, :, cb * QCOLS:(cb + 1) * QCOLS],
                     preferred_element_type=F32)
        if mask is not None:
            st = jnp.where(mask(cb), st, NEG_INF)
        return st, jnp.max(st, axis=0, keepdims=True)

    def update(unit, st_max, vd_cols, vf_cols, n, mode, src, dst, growth):
        u, cb = unit
        st, t_max = st_max
        idx = u * n_cb + cb
        cols = slice(cb * QCOLS, (cb + 1) * QCOLS)
        ref_dst, max_dst = dst[0], dst[1]
        if u < n_dmaps:
            group, a, v_t = 2, u, vd_cols(u // 2, n)
        else:
            group, a, v_t = 3, u - n_dmaps, vf_cols(u - n_dmaps, n)
        outside = None
        if mode == "first":
            shift = t_max
        else:
            r_old = src[0][idx]
            if mode == "fast":
                x_old = src[1][idx]
                shift = x_old + growth[u - n_dmaps] if u >= n_dmaps else x_old
            else:
                shift = jnp.maximum(r_old, t_max)
        p = jnp.exp2(st - shift).astype(BF16)
        pv = jnp.dot(v_t, p, preferred_element_type=F32)
        if mode == "first":
            dst[group][a, :, cols] = pv
            max_dst[idx] = t_max
        else:
            dst[group][a, :, cols] = src[group][a, :, cols] * jnp.exp2(r_old - shift) + pv
            if mode == "fast":
                x_new = jnp.maximum(x_old, t_max)
                max_dst[idx] = x_new
                outside = jnp.maximum(t_max - shift - FAST_OVER, shift - x_new - FAST_UNDER)
            elif mode == "exact":
                max_dst[idx] = jnp.maximum(src[1][idx], t_max)
        ref_dst[idx] = shift
        return outside

    def sweep(k_rows, vd_cols, vf_cols, n_keys, mask, mode, src, dst, growth=None,
              lookahead=SCORE_LOOKAHEAD):
        pending = {w: scores(units[w], k_rows, n_keys(units[w][1]), mask)
                   for w in range(lookahead)}
        worst = None
        for w, unit in enumerate(units):
            nxt = w + lookahead
            if nxt < len(units):
                pending[nxt] = scores(units[nxt], k_rows, n_keys(units[nxt][1]), mask)
            outside = update(unit, pending.pop(w), vd_cols, vf_cols, n_keys(unit[1]),
                             mode, src, dst, growth)
            if outside is not None:
                worst = outside if worst is None else jnp.maximum(worst, outside)
        return worst

    def key_tile(g):
        row0 = pl.multiple_of(g * tk, tk)
        return (lambda p, n: k_ref[0, p, pl.ds(row0, n), :],
                lambda h, n: vd_ref[0, g, h, :, 0:n],
                lambda h, n: vf_ref[0, g, h, :, 0:n])

    for start_par, start_state in ((0, state_a), (1, state_b)):
        @pl.when(jnp.logical_and(kg == 0, qi % 2 == start_par))
        def _():
            sweep(lambda p, n: km_ref[p], lambda h, n: vdm_ref[h], lambda h, n: vfm_ref[h],
                  lambda cb: N_META, None, "first", None, start_state, lookahead=META_LOOKAHEAD)
            par_scr[0] = start_par

    def copy_b_to_a():
        for dst_ref, src_ref in zip(state_a, state_b):
            for i in range(src_ref.shape[0]):
                dst_ref[i] = src_ref[i]

    first = kg * n_group
    cend_row = pl.program_id(0) * (n_tiles + 1) + first

    def full_tile(g, carry):
        par = par_scr[0]
        prev = (cend_row + g) * FOX_HEADS
        growth = [cend_ref[prev + hh] - cend_ref[prev + FOX_HEADS + hh] for hh in range(FOX_HEADS)]

        @pl.when(par == 0)
        def _():
            outside_scr[...] = sweep(*key_tile(g), lambda cb: tk, None, "fast",
                                     state_a, state_b, growth)

        @pl.when(par == 1)
        def _():
            outside_scr[...] = sweep(*key_tile(g), lambda cb: tk, None, "fast",
                                     state_b, state_a, growth)

        redo = jnp.max(outside_scr[...]) > 0.0

        @pl.when(redo)
        def _():
            @pl.when(par == 1)
            def _():
                copy_b_to_a()

            sweep(*key_tile(g), lambda cb: tk, None, "exact", state_a, state_b)

        par_scr[0] = jnp.where(redo, 1, 1 - par)
        return carry

    lax.fori_loop(0, jnp.clip(qi - first, 0, n_group), full_tile, 0)

    @pl.when(jnp.logical_and(qi >= first, qi < first + n_group))
    def _():
        def mask(cb):
            n = (cb + 1) * QCOLS
            key = lax.broadcasted_iota(jnp.int32, (n, QCOLS), 0)
            qry = lax.broadcasted_iota(jnp.int32, (n, QCOLS), 1)
            return key <= qry + cb * QCOLS

        @pl.when(par_scr[0] == 1)
        def _():
            copy_b_to_a()

        sweep(*key_tile(qi - first), lambda cb: (cb + 1) * QCOLS, mask, "last", state_a, state_a,
              lookahead=DIAG_LOOKAHEAD)
        accd_scr, accf_scr = accd_a, accf_a

        lam = (jnp.exp(jnp.sum(lq1_ref[...] * lk1_ref[...], axis=1, keepdims=True))
               - jnp.exp(jnp.sum(lq2_ref[...] * lk2_ref[...], axis=1, keepdims=True))
               + lambda_init)
        for h in range(DIFF_HEADS):
            a1 = accd_scr[2 * h]
            a2 = accd_scr[2 * h + 1]
            o = (a1[0:DIFF_V_DIM] / a1[DIFF_V_DIM:DIFF_V_DIM + 1]
                 - lam * (a2[0:DIFF_V_DIM] / a2[DIFF_V_DIM:DIFF_V_DIM + 1]))
            ms = jnp.mean(o * o, axis=0, keepdims=True)
            o = o * lax.rsqrt(ms + SUBLN_EPS) * subg_ref[...] * (1.0 - lambda_init)
            o_ref[0, :, h * 128:(h + 1) * 128] = o.T.astype(o_ref.dtype)
        for i in range(FOX_HEADS // 2):
            a0 = accf_scr[2 * i]
            a1 = accf_scr[2 * i + 1]
            o = jnp.concatenate(
                [a0[0:FOX_HEAD_DIM] / a0[FOX_HEAD_DIM:FOX_HEAD_DIM + 1],
                 a1[0:FOX_HEAD_DIM] / a1[FOX_HEAD_DIM:FOX_HEAD_DIM + 1]], axis=0)
            c0 = DIFF_HEADS * DIFF_V_DIM + i * 128
            o_ref[0, :, c0:c0 + 128] = o.T.astype(o_ref.dtype)


def _attn(qt, k, vd, vf, km, vdm, vfm, cend, lq1, lk1, lq2, lk2, subg, lambda_init, tile):
    b, nq = qt.shape[0], qt.shape[1]
    s = nq * tile
    grp = min(KV_GROUP, nq)
    assert qt.shape[4] == tile and tile % QCOLS == 0 and nq % grp == 0
    qi_tab = np.concatenate([np.full(i // grp + 1, i, np.int32) for i in range(nq)])
    kg_tab = np.concatenate([np.arange(i // grp + 1, dtype=np.int32) for i in range(nq)])
    n_steps = int(qi_tab.shape[0])
    d_mix = DIFF_HEADS * DIFF_V_DIM + FOX_HEADS * FOX_HEAD_DIM

    def const(shape):
        nd = len(shape)
        return pl.BlockSpec(shape, lambda *_: (0,) * nd)

    n_units = N_QMAPS * (tile // QCOLS)
    grid_spec = pltpu.PrefetchScalarGridSpec(
        num_scalar_prefetch=3,
        grid=(b, n_steps),
        in_specs=[
            pl.BlockSpec((1, 1, N_QMAPS, 128, tile), lambda i, t, qi, kg, ce: (i, qi[t], 0, 0, 0)),
            pl.BlockSpec((1, N_KBLOCKS, grp * tile, 128), lambda i, t, qi, kg, ce: (i, 0, kg[t], 0)),
            pl.BlockSpec((1, grp, DIFF_HEADS, VD_ROWS, tile),
                         lambda i, t, qi, kg, ce: (i, kg[t], 0, 0, 0)),
            pl.BlockSpec((1, grp, FOX_HEADS, VF_ROWS, tile),
                         lambda i, t, qi, kg, ce: (i, kg[t], 0, 0, 0)),
            const(km.shape), const(vdm.shape), const(vfm.shape),
            const(lq1.shape), const(lk1.shape), const(lq2.shape), const(lk2.shape),
            const(subg.shape),
        ],
        out_specs=pl.BlockSpec((1, tile, d_mix), lambda i, t, qi, kg, ce: (i, qi[t], 0)),
        scratch_shapes=2 * [pltpu.VMEM((n_units, 1, QCOLS), F32),
                            pltpu.VMEM((n_units, 1, QCOLS), F32),
                            pltpu.VMEM((2 * DIFF_HEADS, VD_ROWS, tile), F32),
                            pltpu.VMEM((FOX_HEADS, VF_ROWS, tile), F32)]
        + [pltpu.VMEM((1, QCOLS), F32), pltpu.SMEM((1,), jnp.int32)],
    )
    return pl.pallas_call(
        functools.partial(_attn_kernel, lambda_init=lambda_init, n_tiles=nq),
        grid_spec=grid_spec,
        out_shape=jax.ShapeDtypeStruct((b, s, d_mix), BF16),
        compiler_params=pltpu.CompilerParams(dimension_semantics=("parallel", "arbitrary"),
                                             vmem_limit_bytes=VMEM_LIMIT),
        name="attn",
    )(jnp.asarray(qi_tab), jnp.asarray(kg_tab), cend.reshape(-1), qt, k, vd, vf, km, vdm, vfm,
      lq1, lk1, lq2, lk2, subg)


def _placement_matrix():
    m = np.zeros((128, FOX_HEADS * 128), np.float32)
    for p in range(3):
        for h in range(FOX_HEADS):
            m[p * FOX_HEADS + h, h * 128 + FOX_HEAD_DIM + p] = 1.0
    return m


def kernel(x, meta_tokens, ffn1_norm_g, ffn1_w_gate_up, ffn1_w_down, mix_norm_g, w_in, b_forget, lam_q1, lam_k1, lam_q2, lam_k2, diff_subln_g, w_out, ffn2_norm_g, ffn2_w_gate_up, ffn2_w_down, final_norm_g):
    b, s, d = x.shape
    assert ffn1_norm_g.shape[0] == 1, "single-layer problem: meta rows are not carried past the mixer"
    assert s % ATTN_TILE == 0 and ROW_TILE == ATTN_TILE
    layer = 0
    lambda_init = _lambda_init(layer)

    row = lambda v: v.reshape(1, -1).astype(F32)
    wgu1 = ffn1_w_gate_up[layer].astype(BF16)
    wd1 = ffn1_w_down[layer].astype(BF16)
    wgu2 = ffn2_w_gate_up[layer].astype(BF16)
    wd2 = ffn2_w_down[layer].astype(BF16)
    wout = w_out[layer].astype(BF16)
    w = w_in[layer]
    n512 = 512
    dq_w, dk_w, dv_w, fq_w, fk_w, fv_w = (w[:, i * n512:(i + 1) * n512] for i in range(6))
    fl_w = w[:, 6 * n512:]
    fl_blk = jnp.pad(fl_w, ((0, 0), (0, 128 - FOX_HEADS)))
    wrows = jnp.concatenate([dk_w, fk_w, fl_blk], axis=1).astype(BF16)
    q_scale = QK_SCALE * LOG2E
    wt = jnp.concatenate([dq_w * q_scale, fq_w * q_scale, dv_w, fv_w], axis=1).T.astype(BF16)
    bfg = jnp.pad(b_forget[layer].reshape(1, -1).astype(F32), ((0, 0), (0, 128 - FOX_HEADS)))
    place = jnp.asarray(_placement_matrix()).astype(BF16)
    subg = diff_subln_g[layer].reshape(-1, 1).astype(F32)

    n_pos = max(N_META + s, META_TILE)
    pos = jnp.arange(n_pos, dtype=F32)
    inv_freq = jnp.power(ROPE_THETA, -jnp.arange(0, ROPE_DIMS, 2, dtype=F32) / ROPE_DIMS)
    ang = pos[:, None] * inv_freq[None, :]
    cos, sin = jnp.cos(ang), jnp.sin(ang)
    lane_r = np.arange(128) % DIFF_QK_DIM
    in_rope = jnp.asarray(lane_r < ROPE_DIMS)[None, :]
    lane_f = jnp.asarray(lane_r % (ROPE_DIMS // 2))
    cc_all = jnp.where(in_rope, jnp.take(cos, lane_f, axis=1), 1.0)
    sin_sign = jnp.asarray(np.where(lane_r < ROPE_DIMS // 2, -1.0, 1.0).astype(np.float32))[None, :]
    ss_all = jnp.where(in_rope, jnp.take(sin, lane_f, axis=1) * sin_sign, 0.0)
    cost_all, sint_all = cos.T, sin.T

    hm = jnp.pad(meta_tokens.astype(F32), ((0, META_TILE - N_META), (0, 0)))
    hm1 = _ffn1(hm, row(ffn1_norm_g[layer]), wgu1, wd1, META_TILE)
    zero_carry = jnp.zeros((8, 128), F32)
    _, km, vdm, vfm, carry_m = _proj(
        hm1[None], row(mix_norm_g[layer]), wrows, wt, bfg,
        cc_all[:META_TILE], ss_all[:META_TILE], cost_all[:, :META_TILE], sint_all[:, :META_TILE],
        zero_carry, place, META_TILE, N_META)

    h1 = _ffn1(x.reshape(b * s, d), row(ffn1_norm_g[layer]), wgu1, wd1, FFN_TILE)
    qt, k, vd, vf, cum_end = _proj(
        h1.reshape(b, s, d), row(mix_norm_g[layer]), wrows, wt, bfg,
        cc_all[N_META:N_META + s], ss_all[N_META:N_META + s],
        cost_all[:, N_META:N_META + s], sint_all[:, N_META:N_META + s],
        carry_m[0, 0], place, ROW_TILE, ROW_TILE)
    meta_end = jnp.broadcast_to(carry_m[0, 0, 0, :FOX_HEADS], (b, 1, FOX_HEADS))
    cend = jnp.concatenate([meta_end, cum_end[:, :, 0, :FOX_HEADS]], axis=1) * LOG2E
    o = _attn(qt, k, vd, vf, km[0, :, :N_META, :], vdm[0, 0, :, :, :N_META], vfm[0, 0, :, :, :N_META],
              cend, row(lam_q1[layer]), row(lam_k1[layer]), row(lam_q2[layer]), row(lam_k2[layer]),
              subg, lambda_init, ATTN_TILE)
    out = _tail(o.reshape(b * s, -1), h1, wout, row(ffn2_norm_g[layer]), wgu2, wd2,
                row(final_norm_g), FFN_TILE)
    return out.reshape(b, s, d)
```

```python
import functools
import math

import numpy as np
import jax
import jax.numpy as jnp
from jax import lax
from jax.experimental import pallas as pl
from jax.experimental.pallas import tpu as pltpu

F32 = jnp.float32
BF16 = jnp.bfloat16

N_META = 16
DIFF_HEADS = 4
DIFF_V_DIM = 128
DIFF_QK_DIM = 64
FOX_HEADS = 8
FOX_HEAD_DIM = 64
ROPE_DIMS = 16
ROPE_THETA = 500000.0
RMS_EPS = 1e-6
SUBLN_EPS = 1e-5
NEG_INF = -1e30
QK_SCALE = 0.125
LOG2E = math.log2(math.e)

N_QMAPS = 2 * DIFF_HEADS + FOX_HEADS
N_KBLOCKS = DIFF_HEADS + FOX_HEADS
BF16_SUBLANES = 16
VD_ROWS = DIFF_V_DIM + BF16_SUBLANES
VF_ROWS = FOX_HEAD_DIM + BF16_SUBLANES
META_TILE = 128
FFN_CHUNK = 256
ROW_TILE = 512
FFN_TILE = 1024
ATTN_TILE = 512
QCOLS = 256
FAST_OVER = 20.0
FAST_UNDER = 40.0
KV_GROUP = 4
SCORE_LOOKAHEAD = 3
DIAG_LOOKAHEAD = 5
META_LOOKAHEAD = 12
V7X_VMEM_BYTES = 64 * 1024 * 1024
VMEM_LIMIT = V7X_VMEM_BYTES - 8 * 1024 * 1024


def _lambda_init(layer_idx):
    return 0.8 - 0.6 * math.exp(-0.3 * layer_idx)


def _rms(x, g, eps):
    ms = jnp.mean(x * x, axis=-1, keepdims=True)
    return x * lax.rsqrt(ms + eps) * g


def _split3(x):
    hi = x.astype(BF16)
    r = x - hi.astype(F32)
    mid = r.astype(BF16)
    lo = (r - mid.astype(F32)).astype(BF16)
    return hi, mid, lo


def _const_spec(shape):
    nd = len(shape)
    return pl.BlockSpec(shape, lambda *_: (0,) * nd, pipeline_mode=pl.Buffered(1))


def _swiglu_half_step(h, g_ref, wgu_ref, wd_ref):
    d_ff = wd_ref.shape[0]
    hn = _rms(h, g_ref[...], RMS_EPS).astype(BF16)
    acc = jnp.zeros(h.shape, F32)
    for c in range(d_ff // FFN_CHUNK):
        lo, hi = c * FFN_CHUNK, (c + 1) * FFN_CHUNK
        gate = jnp.dot(hn, wgu_ref[:, lo:hi], preferred_element_type=F32)
        up = jnp.dot(hn, wgu_ref[:, d_ff + lo:d_ff + hi], preferred_element_type=F32)
        act = (gate / (1.0 + jnp.exp(-gate)) * up).astype(BF16)
        acc = acc + jnp.dot(act, wd_ref[lo:hi, :], preferred_element_type=F32)
    return h + 0.5 * acc


def _ffn1_kernel(h_ref, g_ref, wgu_ref, wd_ref, o_ref):
    o_ref[...] = _swiglu_half_step(h_ref[...], g_ref, wgu_ref, wd_ref)


def _ffn1(h, g, wgu, wd, tm):
    rows, d = h.shape
    return pl.pallas_call(
        _ffn1_kernel,
        grid=(rows // tm,),
        in_specs=[pl.BlockSpec((tm, d), lambda i: (i, 0)),
                  _const_spec(g.shape), _const_spec(wgu.shape), _const_spec(wd.shape)],
        out_specs=pl.BlockSpec((tm, d), lambda i: (i, 0)),
        out_shape=jax.ShapeDtypeStruct((rows, d), F32),
        compiler_params=pltpu.CompilerParams(dimension_semantics=("parallel",),
                                             vmem_limit_bytes=VMEM_LIMIT),
        name="ffn1",
    )(h, g, wgu, wd)


def _tail_kernel(o_ref, h_ref, wout_ref, g_ref, wgu_ref, wd_ref, gf_ref, out_ref):
    h2 = h_ref[...] + jnp.dot(o_ref[...], wout_ref[...], preferred_element_type=F32)
    h3 = _swiglu_half_step(h2, g_ref, wgu_ref, wd_ref)
    out_ref[...] = _rms(h3, gf_ref[...], RMS_EPS)


def _tail(o, h1, wout, g, wgu, wd, gf, tm):
    rows, d = h1.shape
    return pl.pallas_call(
        _tail_kernel,
        grid=(rows // tm,),
        in_specs=[pl.BlockSpec((tm, d), lambda i: (i, 0)),
                  pl.BlockSpec((tm, d), lambda i: (i, 0)),
                  _const_spec(wout.shape), _const_spec(g.shape), _const_spec(wgu.shape),
                  _const_spec(wd.shape), _const_spec(gf.shape)],
        out_specs=pl.BlockSpec((tm, d), lambda i: (i, 0)),
        out_shape=jax.ShapeDtypeStruct((rows, d), F32),
        compiler_params=pltpu.CompilerParams(dimension_semantics=("parallel",),
                                             vmem_limit_bytes=VMEM_LIMIT),
        name="tail",
    )(o, h1, wout, g, wgu, wd, gf)


def _proj_kernel(h_ref, g_ref, wrows_ref, wt_ref, bf_ref, cc_ref, ss_ref, cost_ref, sint_ref,
                 carry0_ref, place_ref,
                 qt_ref, k_ref, vd_ref, vf_ref, cend_ref, carry_scr, *, n_valid):
    tm = h_ref.shape[1]

    @pl.when(pl.program_id(1) == 0)
    def _():
        carry_scr[...] = carry0_ref[...]

    hn = _rms(h_ref[0], g_ref[...], RMS_EPS).astype(BF16)
    kr = jnp.dot(hn, wrows_ref[...], preferred_element_type=F32)
    pt = lax.dot_general(wt_ref[...], hn, (((1,), (1,)), ((), ())),
                         preferred_element_type=F32)

    lane = lax.broadcasted_iota(jnp.int32, (tm, 128), 1)
    half = ROPE_DIMS // 2
    first_group = (lane % DIFF_QK_DIM) < half
    cc = cc_ref[...]
    ss_from_above = jnp.where(first_group, ss_ref[...], 0.0)
    ss_from_below = jnp.where(first_group, 0.0, ss_ref[...])
    n_dk = DIFF_HEADS * 128
    for p in range(DIFF_HEADS):
        kd = kr[:, p * 128:(p + 1) * 128]
        k_ref[0, p] = (kd * cc + pltpu.roll(kd, 128 - half, 1) * ss_from_above
                       + pltpu.roll(kd, half, 1) * ss_from_below).astype(BF16)

    fk0 = n_dk
    fl0 = fk0 + FOX_HEADS * FOX_HEAD_DIM
    fl = kr[:, fl0:fl0 + 128] + bf_ref[...]
    logf = jnp.minimum(fl, 0.0) - jnp.log(1.0 + jnp.exp(-jnp.abs(fl)))
    keep = lane < FOX_HEADS
    if n_valid < tm:
        row = lax.broadcasted_iota(jnp.int32, (tm, 128), 0)
        keep = jnp.logical_and(keep, row < n_valid)
    logf = jnp.where(keep, logf, 0.0)
    parts = jnp.concatenate(_split3(logf), axis=1)
    r_i = lax.broadcasted_iota(jnp.int32, (tm, tm), 0)
    c_i = lax.broadcasted_iota(jnp.int32, (tm, tm), 1)
    tri = jnp.where(r_i >= c_i, 1.0, 0.0).astype(BF16)
    cs = jnp.dot(tri, parts, preferred_element_type=F32)
    cum = cs[:, 0:128] + cs[:, 128:256] + cs[:, 256:384] + carry_scr[0:1, :]
    carry_scr[...] = jnp.broadcast_to(cum[tm - 1:tm, :], carry_scr.shape)
    cend_ref[0, 0] = carry_scr[...]
    chi, cmid, clo = _split3(cum * LOG2E)
    cparts = (chi.astype(F32) + pltpu.roll(cmid.astype(F32), FOX_HEADS, 1)
              + pltpu.roll(clo.astype(F32), 2 * FOX_HEADS, 1)).astype(BF16)
    aug = jnp.dot(cparts, place_ref[...], preferred_element_type=F32)
    low_half = lane < FOX_HEAD_DIM
    for pr in range(FOX_HEADS // 2):
        pair = kr[:, fk0 + pr * 128:fk0 + (pr + 1) * 128]
        for c, blk in enumerate((pair, pltpu.roll(pair, FOX_HEAD_DIM, 1))):
            hh = 2 * pr + c
            kf = jnp.where(low_half, blk, aug[:, hh * 128:(hh + 1) * 128])
            k_ref[0, DIFF_HEADS + hh] = kf.astype(BF16)

    cost = cost_ref[...]
    sint = sint_ref[...]
    half = ROPE_DIMS // 2
    zeros64 = jnp.zeros((64, tm), BF16)
    for h in range(DIFF_HEADS):
        for c in range(2):
            base = h * 128 + c * 64
            x1 = pt[base:base + half]
            x2 = pt[base + half:base + ROPE_DIMS]
            blk = jnp.concatenate([x1 * cost - x2 * sint, x2 * cost + x1 * sint,
                                   pt[base + ROPE_DIMS:base + 64]], axis=0).astype(BF16)
            u = 2 * h + c
            qt_ref[0, 0, u, c * 64:(c + 1) * 64, :] = blk
            qt_ref[0, 0, u, (1 - c) * 64:(2 - c) * 64, :] = zeros64
    fq0 = DIFF_HEADS * 128
    rows64 = lax.broadcasted_iota(jnp.int32, (64, tm), 0)
    neg_rows = jnp.where(rows64 < 3, -1.0, 0.0).astype(BF16)
    for hh in range(FOX_HEADS):
        u = 2 * DIFF_HEADS + hh
        qt_ref[0, 0, u, 0:64, :] = pt[fq0 + hh * 64:fq0 + (hh + 1) * 64].astype(BF16)
        qt_ref[0, 0, u, 64:128, :] = neg_rows

    ones_rows = jnp.ones((BF16_SUBLANES, tm), BF16)
    dv0 = fq0 + FOX_HEADS * 64
    for h in range(DIFF_HEADS):
        vd_ref[0, 0, h, 0:DIFF_V_DIM, :] = pt[dv0 + h * 128:dv0 + (h + 1) * 128].astype(BF16)
        vd_ref[0, 0, h, DIFF_V_DIM:VD_ROWS, :] = ones_rows
    fv0 = dv0 + DIFF_HEADS * 128
    for hh in range(FOX_HEADS):
        vf_ref[0, 0, hh, 0:FOX_HEAD_DIM, :] = pt[fv0 + hh * 64:fv0 + (hh + 1) * 64].astype(BF16)
        vf_ref[0, 0, hh, FOX_HEAD_DIM:VF_ROWS, :] = ones_rows


def _proj(h, g, wrows, wt, bfg, cc, ss, cost, sint, carry0, place, tm, n_valid):
    b, s, d = h.shape
    nt = s // tm
    out_shape = (
        jax.ShapeDtypeStruct((b, nt, N_QMAPS, 128, tm), BF16),
        jax.ShapeDtypeStruct((b, N_KBLOCKS, s, 128), BF16),
        jax.ShapeDtypeStruct((b, nt, DIFF_HEADS, VD_ROWS, tm), BF16),
        jax.ShapeDtypeStruct((b, nt, FOX_HEADS, VF_ROWS, tm), BF16),
        jax.ShapeDtypeStruct((b, nt, 8, 128), F32),
    )
    return pl.pallas_call(
        functools.partial(_proj_kernel, n_valid=n_valid),
        grid=(b, nt),
        in_specs=[pl.BlockSpec((1, tm, d), lambda i, t: (i, t, 0)),
                  _const_spec(g.shape), _const_spec(wrows.shape), _const_spec(wt.shape),
                  _const_spec(bfg.shape),
                  pl.BlockSpec((tm, 128), lambda i, t: (t, 0)),
                  pl.BlockSpec((tm, 128), lambda i, t: (t, 0)),
                  pl.BlockSpec((8, tm), lambda i, t: (0, t)),
                  pl.BlockSpec((8, tm), lambda i, t: (0, t)),
                  _const_spec(carry0.shape), _const_spec(place.shape)],
        out_specs=(pl.BlockSpec((1, 1, N_QMAPS, 128, tm), lambda i, t: (i, t, 0, 0, 0)),
                   pl.BlockSpec((1, N_KBLOCKS, tm, 128), lambda i, t: (i, 0, t, 0)),
                   pl.BlockSpec((1, 1, DIFF_HEADS, VD_ROWS, tm), lambda i, t: (i, t, 0, 0, 0)),
                   pl.BlockSpec((1, 1, FOX_HEADS, VF_ROWS, tm), lambda i, t: (i, t, 0, 0, 0)),
                   pl.BlockSpec((1, 1, 8, 128), lambda i, t: (i, t, 0, 0))),
        out_shape=out_shape,
        scratch_shapes=[pltpu.VMEM((8, 128), F32)],
        compiler_params=pltpu.CompilerParams(dimension_semantics=("parallel", "arbitrary"),
                                             vmem_limit_bytes=VMEM_LIMIT),
        name="proj",
    )(h, g, wrows, wt, bfg, cc, ss, cost, sint, carry0, place)


def _attn_kernel(qi_ref, kg_ref, cend_ref,
                 qt_ref, k_ref, vd_ref, vf_ref, km_ref, vdm_ref, vfm_ref,
                 lq1_ref, lk1_ref, lq2_ref, lk2_ref, subg_ref,
                 o_ref, ref_a, max_a, accd_a, accf_a, ref_b, max_b, accd_b, accf_b,
                 outside_scr, par_scr, *, lambda_init, n_tiles):
    state_a = (ref_a, max_a, accd_a, accf_a)
    state_b = (ref_b, max_b, accd_b, accf_b)
    s_idx = pl.program_id(1)
    qi = qi_ref[s_idx]
    kg = kg_ref[s_idx]
    tq = qt_ref.shape[4]
    n_group = vd_ref.shape[1]
    tk = k_ref.shape[2] // n_group
    assert tq % QCOLS == 0 and tq == tk

    n_dmaps = 2 * DIFF_HEADS
    n_cb = tq // QCOLS
    units = [(u, cb) for u in range(N_QMAPS) for cb in range(n_cb)]

    def scores(unit, k_rows, n, mask):
        u, cb = unit
        kb = u // 2 if u < n_dmaps else u - DIFF_HEADS
        st = jnp.dot(k_rows(kb, n), qt_ref[0, 0, u, :, cb * QCOLS:(cb + 1) * QCOLS],
                     preferred_element_type=F32)
        if mask is not None:
            st = jnp.where(mask(cb), st, NEG_INF)
        return st, jnp.max(st, axis=0, keepdims=True)

    def update(unit, st_max, vd_cols, vf_cols, n, mode, src, dst, growth):
        u, cb = unit
        st, t_max = st_max
        idx = u * n_cb + cb
        cols = slice(cb * QCOLS, (cb + 1) * QCOLS)
        ref_dst, max_dst = dst[0], dst[1]
        if u < n_dmaps:
            group, a, v_t = 2, u, vd_cols(u // 2, n)
        else:
            group, a, v_t = 3, u - n_dmaps, vf_cols(u - n_dmaps, n)
        outside = None
        if mode == "first":
            shift = t_max
        else:
            r_old = src[0][idx]
            if mode == "fast":
                x_old = src[1][idx]
                shift = x_old + growth[u - n_dmaps] if u >= n_dmaps else x_old
            else:
                shift = jnp.maximum(r_old, t_max)
        p = jnp.exp2(st - shift).astype(BF16)
        pv = jnp.dot(v_t, p, preferred_element_type=F32)
        if mode == "first":
            dst[group][a, :, cols] = pv
            max_dst[idx] = t_max
        else:
            dst[group][a, :, cols] = src[group][a, :, cols] * jnp.exp2(r_old - shift) + pv
            if mode == "fast":
                x_new = jnp.maximum(x_old, t_max)
                max_dst[idx] = x_new
                outside = jnp.maximum(t_max - shift - FAST_OVER, shift - x_new - FAST_UNDER)
            elif mode == "exact":
                max_dst[idx] = jnp.maximum(src[1][idx], t_max)
        ref_dst[idx] = shift
        return outside

    def sweep(k_rows, vd_cols, vf_cols, n_keys, mask, mode, src, dst, growth=None,
              lookahead=SCORE_LOOKAHEAD):
        pending = {w: scores(units[w], k_rows, n_keys(units[w][1]), mask)
                   for w in range(lookahead)}
        worst = None
        for w, unit in enumerate(units):
            nxt = w + lookahead
            if nxt < len(units):
                pending[nxt] = scores(units[nxt], k_rows, n_keys(units[nxt][1]), mask)
            outside = update(unit, pending.pop(w), vd_cols, vf_cols, n_keys(unit[1]),
                             mode, src, dst, growth)
            if outside is not None:
                worst = outside if worst is None else jnp.maximum(worst, outside)
        return worst

    def key_tile(g):
        row0 = pl.multiple_of(g * tk, tk)
        return (lambda p, n: k_ref[0, p, pl.ds(row0, n), :],
                lambda h, n: vd_ref[0, g, h, :, 0:n],
                lambda h, n: vf_ref[0, g, h, :, 0:n])

    for start_par, start_state in ((0, state_a), (1, state_b)):
        @pl.when(jnp.logical_and(kg == 0, qi % 2 == start_par))
        def _():
            sweep(lambda p, n: km_ref[p], lambda h, n: vdm_ref[h], lambda h, n: vfm_ref[h],
                  lambda cb: N_META, None, "first", None, start_state, lookahead=META_LOOKAHEAD)
            par_scr[0] = start_par

    def copy_b_to_a():
        for dst_ref, src_ref in zip(state_a, state_b):
            for i in range(src_ref.shape[0]):
                dst_ref[i] = src_ref[i]

    first = kg * n_group
    cend_row = pl.program_id(0) * (n_tiles + 1) + first

    def full_tile(g, carry):
        par = par_scr[0]
        prev = (cend_row + g) * FOX_HEADS
        growth = [cend_ref[prev + hh] - cend_ref[prev + FOX_HEADS + hh] for hh in range(FOX_HEADS)]

        @pl.when(par == 0)
        def _():
            outside_scr[...] = sweep(*key_tile(g), lambda cb: tk, None, "fast",
                                     state_a, state_b, growth)

        @pl.when(par == 1)
        def _():
            outside_scr[...] = sweep(*key_tile(g), lambda cb: tk, None, "fast",
                                     state_b, state_a, growth)

        redo = jnp.max(outside_scr[...]) > 0.0

        @pl.when(redo)
        def _():
            @pl.when(par == 1)
            def _():
                copy_b_to_a()

            sweep(*key_tile(g), lambda cb: tk, None, "exact", state_a, state_b)

        par_scr[0] = jnp.where(redo, 1, 1 - par)
        return carry

    lax.fori_loop(0, jnp.clip(qi - first, 0, n_group), full_tile, 0)

    @pl.when(jnp.logical_and(qi >= first, qi < first + n_group))
    def _():
        def mask(cb):
            n = (cb + 1) * QCOLS
            key = lax.broadcasted_iota(jnp.int32, (n, QCOLS), 0)
            qry = lax.broadcasted_iota(jnp.int32, (n, QCOLS), 1)
            return key <= qry + cb * QCOLS

        @pl.when(par_scr[0] == 1)
        def _():
            copy_b_to_a()

        sweep(*key_tile(qi - first), lambda cb: (cb + 1) * QCOLS, mask, "last", state_a, state_a,
              lookahead=DIAG_LOOKAHEAD)
        accd_scr, accf_scr = accd_a, accf_a

        lam = (jnp.exp(jnp.sum(lq1_ref[...] * lk1_ref[...], axis=1, keepdims=True))
               - jnp.exp(jnp.sum(lq2_ref[...] * lk2_ref[...], axis=1, keepdims=True))
               + lambda_init)
        for h in range(DIFF_HEADS):
            a1 = accd_scr[2 * h]
            a2 = accd_scr[2 * h + 1]
            o = (a1[0:DIFF_V_DIM] / a1[DIFF_V_DIM:DIFF_V_DIM + 1]
                 - lam * (a2[0:DIFF_V_DIM] / a2[DIFF_V_DIM:DIFF_V_DIM + 1]))
            ms = jnp.mean(o * o, axis=0, keepdims=True)
            o = o * lax.rsqrt(ms + SUBLN_EPS) * subg_ref[...] * (1.0 - lambda_init)
            o_ref[0, :, h * 128:(h + 1) * 128] = o.T.astype(o_ref.dtype)
        for i in range(FOX_HEADS // 2):
            a0 = accf_scr[2 * i]
            a1 = accf_scr[2 * i + 1]
            o = jnp.concatenate(
                [a0[0:FOX_HEAD_DIM] / a0[FOX_HEAD_DIM:FOX_HEAD_DIM + 1],
                 a1[0:FOX_HEAD_DIM] / a1[FOX_HEAD_DIM:FOX_HEAD_DIM + 1]], axis=0)
            c0 = DIFF_HEADS * DIFF_V_DIM + i * 128
            o_ref[0, :, c0:c0 + 128] = o.T.astype(o_ref.dtype)


def _attn(qt, k, vd, vf, km, vdm, vfm, cend, lq1, lk1, lq2, lk2, subg, lambda_init, tile):
    b, nq = qt.shape[0], qt.shape[1]
    s = nq * tile
    grp = min(KV_GROUP, nq)
    assert qt.shape[4] == tile and tile % QCOLS == 0 and nq % grp == 0
    qi_tab = np.concatenate([np.full(i // grp + 1, i, np.int32) for i in range(nq)])
    kg_tab = np.concatenate([np.arange(i // grp + 1, dtype=np.int32) for i in range(nq)])
    n_steps = int(qi_tab.shape[0])
    d_mix = DIFF_HEADS * DIFF_V_DIM + FOX_HEADS * FOX_HEAD_DIM

    def const(shape):
        nd = len(shape)
        return pl.BlockSpec(shape, lambda *_: (0,) * nd)

    n_units = N_QMAPS * (tile // QCOLS)
    grid_spec = pltpu.PrefetchScalarGridSpec(
        num_scalar_prefetch=3,
        grid=(b, n_steps),
        in_specs=[
            pl.BlockSpec((1, 1, N_QMAPS, 128, tile), lambda i, t, qi, kg, ce: (i, qi[t], 0, 0, 0)),
            pl.BlockSpec((1, N_KBLOCKS, grp * tile, 128), lambda i, t, qi, kg, ce: (i, 0, kg[t], 0)),
            pl.BlockSpec((1, grp, DIFF_HEADS, VD_ROWS, tile),
                         lambda i, t, qi, kg, ce: (i, kg[t], 0, 0, 0)),
            pl.BlockSpec((1, grp, FOX_HEADS, VF_ROWS, tile),
                         lambda i, t, qi, kg, ce: (i, kg[t], 0, 0, 0)),
            const(km.shape), const(vdm.shape), const(vfm.shape),
            const(lq1.shape), const(lk1.shape), const(lq2.shape), const(lk2.shape),
            const(subg.shape),
        ],
        out_specs=pl.BlockSpec((1, tile, d_mix), lambda i, t, qi, kg, ce: (i, qi[t], 0)),
        scratch_shapes=2 * [pltpu.VMEM((n_units, 1, QCOLS), F32),
                            pltpu.VMEM((n_units, 1, QCOLS), F32),
                            pltpu.VMEM((2 * DIFF_HEADS, VD_ROWS, tile), F32),
                            pltpu.VMEM((FOX_HEADS, VF_ROWS, tile), F32)]
        + [pltpu.VMEM((1, QCOLS), F32), pltpu.SMEM((1,), jnp.int32)],
    )
    return pl.pallas_call(
        functools.partial(_attn_kernel, lambda_init=lambda_init, n_tiles=nq),
        grid_spec=grid_spec,
        out_shape=jax.ShapeDtypeStruct((b, s, d_mix), BF16),
        compiler_params=pltpu.CompilerParams(dimension_semantics=("parallel", "arbitrary"),
                                             vmem_limit_bytes=VMEM_LIMIT),
        name="attn",
    )(jnp.asarray(qi_tab), jnp.asarray(kg_tab), cend.reshape(-1), qt, k, vd, vf, km, vdm, vfm,
      lq1, lk1, lq2, lk2, subg)


def _placement_matrix():
    m = np.zeros((128, FOX_HEADS * 128), np.float32)
    for p in range(3):
        for h in range(FOX_HEADS):
            m[p * FOX_HEADS + h, h * 128 + FOX_HEAD_DIM + p] = 1.0
    return m


def kernel(x, meta_tokens, ffn1_norm_g, ffn1_w_gate_up, ffn1_w_down, mix_norm_g, w_in, b_forget, lam_q1, lam_k1, lam_q2, lam_k2, diff_subln_g, w_out, ffn2_norm_g, ffn2_w_gate_up, ffn2_w_down, final_norm_g):
    b, s, d = x.shape
    assert ffn1_norm_g.shape[0] == 1, "single-layer problem: meta rows are not carried past the mixer"
    assert s % ATTN_TILE == 0 and ROW_TILE == ATTN_TILE
    layer = 0
    lambda_init = _lambda_init(layer)

    row = lambda v: v.reshape(1, -1).astype(F32)
    wgu1 = ffn1_w_gate_up[layer].astype(BF16)
    wd1 = ffn1_w_down[layer].astype(BF16)
    wgu2 = ffn2_w_gate_up[layer].astype(BF16)
    wd2 = ffn2_w_down[layer].astype(BF16)
    wout = w_out[layer].astype(BF16)
    w = w_in[layer]
    n512 = 512
    dq_w, dk_w, dv_w, fq_w, fk_w, fv_w = (w[:, i * n512:(i + 1) * n512] for i in range(6))
    fl_w = w[:, 6 * n512:]
    fl_blk = jnp.pad(fl_w, ((0, 0), (0, 128 - FOX_HEADS)))
    wrows = jnp.concatenate([dk_w, fk_w, fl_blk], axis=1).astype(BF16)
    q_scale = QK_SCALE * LOG2E
    wt = jnp.concatenate([dq_w * q_scale, fq_w * q_scale, dv_w, fv_w], axis=1).astype(BF16).T
    bfg = jnp.pad(b_forget[layer].reshape(1, -1).astype(F32), ((0, 0), (0, 128 - FOX_HEADS)))
    place = jnp.asarray(_placement_matrix()).astype(BF16)
    subg = diff_subln_g[layer].reshape(-1, 1).astype(F32)

    n_pos = max(N_META + s, META_TILE)
    pos = jnp.arange(n_pos, dtype=F32)
    inv_freq = jnp.power(ROPE_THETA, -jnp.arange(0, ROPE_DIMS, 2, dtype=F32) / ROPE_DIMS)
    ang = pos[:, None] * inv_freq[None, :]
    cos, sin = jnp.cos(ang), jnp.sin(ang)
    rest = (n_pos, DIFF_QK_DIM - ROPE_DIMS)
    cc_map = jnp.concatenate([cos, cos, jnp.ones(rest, F32)], axis=1)
    ss_map = jnp.concatenate([-sin, sin, jnp.zeros(rest, F32)], axis=1)
    cc_all = jnp.concatenate([cc_map, cc_map], axis=1)
    ss_all = jnp.concatenate([ss_map, ss_map], axis=1)
    cost_all, sint_all = cos.T, sin.T

    hm = jnp.pad(meta_tokens.astype(F32), ((0, META_TILE - N_META), (0, 0)))
    hm1 = _ffn1(hm, row(ffn1_norm_g[layer]), wgu1, wd1, META_TILE)
    zero_carry = jnp.zeros((8, 128), F32)
    _, km, vdm, vfm, carry_m = _proj(
        hm1[None], row(mix_norm_g[layer]), wrows, wt, bfg,
        cc_all[:META_TILE], ss_all[:META_TILE], cost_all[:, :META_TILE], sint_all[:, :META_TILE],
        zero_carry, place, META_TILE, N_META)

    h1 = _ffn1(x.reshape(b * s, d), row(ffn1_norm_g[layer]), wgu1, wd1, FFN_TILE)
    qt, k, vd, vf, cum_end = _proj(
        h1.reshape(b, s, d), row(mix_norm_g[layer]), wrows, wt, bfg,
        cc_all[N_META:N_META + s], ss_all[N_META:N_META + s],
        cost_all[:, N_META:N_META + s], sint_all[:, N_META:N_META + s],
        carry_m[0, 0], place, ROW_TILE, ROW_TILE)
    meta_end = jnp.broadcast_to(carry_m[0, 0, 0, :FOX_HEADS], (b, 1, FOX_HEADS))
    cend = jnp.concatenate([meta_end, cum_end[:, :, 0, :FOX_HEADS]], axis=1) * LOG2E
    o = _attn(qt, k, vd, vf, km[0, :, :N_META, :], vdm[0, 0, :, :, :N_META], vfm[0, 0, :, :, :N_META],
              cend, row(lam_q1[layer]), row(lam_k1[layer]), row(lam_q2[layer]), row(lam_k2[layer]),
              subg, lambda_init, ATTN_TILE)
    out = _tail(o.reshape(b * s, -1), h1, wout, row(ffn2_norm_g[layer]), wgu2, wd2,
                row(final_norm_g), FFN_TILE)
    return out.reshape(b, s, d)
```
